```python
import math
import jax
import jax.numpy as jnp
from jax import lax
import numpy as np

D_MODEL = 2048
BATCH = 4
SEQ = 8192
DEPTH = 4

GRID_W = 64
CTX_LEN = 256
N_MIXERS = 3
EPS = 1e-6
D_FF = 4 * D_MODEL
N_A = (DEPTH + 2) // 3
N_B = (DEPTH + 1) // 3
N_C = DEPTH // 3

D_A = D_MODEL
DK_A = 128
H_A = D_A // DK_A
CHUNK_A = 32

DH_B = 128
HQ_B = D_MODEL // DH_B
HKV_B = HQ_B // 4
GRP_B = HQ_B // HKV_B
QW_B = HQ_B * DH_B
KW_B = HKV_B * DH_B
WINDOW_B = 128
BLK_B = 128
SCALE_B = DH_B ** -0.5
ROPE_BASE = 10000.0
ROPE_PAIRS = DH_B // 4

D_INNER_C = 2 * D_MODEL
P_C = 64
H_C = D_INNER_C // P_C
N_GROUPS_C = 8
HPG_C = H_C // N_GROUPS_C
N_STATE_C = 128
GN_C = N_GROUPS_C * N_STATE_C
CONV_C = 5
CONV_CH_C = D_INNER_C + 2 * GN_C
IN_C = D_INNER_C + CONV_CH_C + 2 * H_C
CHUNK_C = 128
DT_MIN = 1e-3
DT_MAX = 1e-1

kernel_name = 'hybrid_hgrn2_swa_ssd_flow_block'


def _rmsnorm(x, g):
    xf = x.astype(jnp.float32)
    y = xf * lax.rsqrt(jnp.mean(xf * xf, axis=-1, keepdims=True) + EPS)
    return (y * g.astype(jnp.float32)).astype(x.dtype)


def _modulate(h, g, shift, scale):
    return _rmsnorm(h, g) * (1.0 + scale) + shift


def _mlp(u, w1, w2):
    return jnp.square(jax.nn.relu(u @ w1)) @ w2


def _identity(a):
    return a


def _flip_t(a):
    return jnp.flip(a, axis=1)


def _bidir_scan(scan_fn, init, ctx_dirs, lat_dirs):
    y_c, y_l = None, None
    for d in range(2):
        fl = _flip_t if d else _identity
        o_c, s_c = scan_fn(init, *[fl(a) for a in ctx_dirs[d]])
        o_l, _ = scan_fn(s_c, *[fl(a) for a in lat_dirs[d]])
        y_l = fl(o_l) if y_l is None else y_l + fl(o_l)
        if o_c is not None:
            y_c = fl(o_c) if y_c is None else y_c + fl(o_c)
    return y_c, y_l


def _glr_scan(s0, k, v, logf, q=None):
    emit = q is not None
    bsz, t_len, heads, _ = k.shape
    n = t_len // CHUNK_A

    def chunks(a):
        return a.reshape(bsz, n, CHUNK_A, heads, a.shape[-1]).transpose(1, 0, 3, 2, 4)

    tri = jnp.tril(jnp.ones((CHUNK_A, CHUNK_A), dtype=bool))[:, :, None]

    def step(S, xs):
        kc, vc, lc = xs[:3]
        G = jnp.cumsum(lc, axis=2)
        G_end = G[:, :, -1]
        S_new = jnp.exp(G_end)[..., None] * S + jnp.einsum(
            'bhcd,bhce->bhde', kc * jnp.exp(G_end[:, :, None] - G), vc)
        if not emit:
            return S_new, None
        qc = xs[3]
        decay = jnp.exp(jnp.where(tri, G[:, :, :, None] - G[:, :, None], -jnp.inf))
        attn = jnp.einsum('bhtd,bhtsd,bhsd->bhts', qc, decay, kc)
        o = jnp.einsum('bhts,bhse->bhte', attn, vc) + jnp.einsum('bhtd,bhde->bhte', qc * jnp.exp(G), S)
        return S_new, o

    xs = tuple(chunks(a) for a in ((k, v, logf, q) if emit else (k, v, logf)))
    S_fin, o = lax.scan(step, s0, xs)
    if emit:
        o = o.transpose(1, 0, 3, 2, 4).reshape(bsz, t_len, heads, -1)
    return o, S_fin


def _ssd_scan(h0, x, dt, la, bm, cm=None):
    emit = cm is not None
    bsz, t_len = x.shape[:2]
    n = t_len // CHUNK_C

    def chunks(a):
        return jnp.moveaxis(a.reshape(bsz, n, CHUNK_C, *a.shape[2:]), 1, 0)

    tri = jnp.tril(jnp.ones((CHUNK_C, CHUNK_C), dtype=bool))

    def step(h, xs):
        xc, dtc, lac, bc = xs[:4]
        cum = jnp.cumsum(lac, axis=1)
        cum_end = cum[:, -1]
        h_new = jnp.exp(cum_end)[..., None, None] * h + jnp.einsum(
            'bsgn,bsge,bsgep->bgepn', bc, jnp.exp(cum_end[:, None] - cum) * dtc, xc)
        if not emit:
            return h_new, None
        cc = xs[4]
        cum_t = jnp.moveaxis(cum, 1, -1)
        seg = jnp.exp(jnp.where(tri, cum_t[..., :, None] - cum_t[..., None, :], -jnp.inf))
        cb = jnp.einsum('btgn,bsgn->bgts', cc, bc)
        y = jnp.einsum('bgts,bgets,bsge,bsgep->btgep', cb, seg, dtc, xc)
        y = y + jnp.einsum('btgn,bgepn,btge->btgep', cc, h, jnp.exp(cum))
        return h_new, y

    xs = tuple(chunks(a) for a in ((x, dt, la, bm, cm) if emit else (x, dt, la, bm)))
    h_fin, y = lax.scan(step, h0, xs)
    if emit:
        y = jnp.moveaxis(y, 0, 1).reshape(x.shape)
    return y, h_fin


def _dwconv_centred(u, w, b):
    ch = u.shape[-1]
    y = lax.conv_general_dilated(u, w[:, None, :], window_strides=(1,),
                                 padding=((CONV_C // 2, CONV_C // 2),),
                                 dimension_numbers=('NWC', 'WIO', 'NWC'),
                                 feature_group_count=ch)
    return y + b


def _rope_axis(x, ang):
    cos = jnp.cos(ang)[None, :, None, :]
    sin = jnp.sin(ang)[None, :, None, :]
    x1, x2 = jnp.split(x, 2, axis=-1)
    return jnp.concatenate([x1 * cos - x2 * sin, x2 * cos + x1 * sin], axis=-1)


def _rope_2d(x, ang_row, ang_col):
    xr, xc = jnp.split(x, 2, axis=-1)
    return jnp.concatenate([_rope_axis(xr, ang_row), _rope_axis(xc, ang_col)], axis=-1).astype(x.dtype)


def _sink_softmax(s, sink):
    sk = jnp.broadcast_to(sink[None, :, :, None, None], s.shape[:-1] + (1,))
    return jax.nn.softmax(jnp.concatenate([s, sk], axis=-1), axis=-1)[..., :-1]


def _banded_attention(q, k, v, kc, vc, sink):
    bsz, t_len = q.shape[:2]
    nb = t_len // BLK_B
    pad = ((0, 0), (BLK_B, BLK_B), (0, 0), (0, 0))
    kp = jnp.pad(k, pad).reshape(bsz, nb + 2, BLK_B, HKV_B, DH_B)
    vp = jnp.pad(v, pad).reshape(bsz, nb + 2, BLK_B, HKV_B, DH_B)
    kband = jnp.concatenate([kp[:, :-2], kp[:, 1:-1], kp[:, 2:]], axis=2)
    vband = jnp.concatenate([vp[:, :-2], vp[:, 1:-1], vp[:, 2:]], axis=2)
    qb = q.reshape(bsz, nb, BLK_B, HKV_B, GRP_B, DH_B)
    offs = jnp.arange(3 * BLK_B) - BLK_B
    qoff = jnp.arange(BLK_B)

    def block(args):
        qj, kj, vj, j = args
        kpos = j * BLK_B + offs
        qpos = j * BLK_B + qoff
        valid = (jnp.abs(kpos[None, :] - qpos[:, None]) <= WINDOW_B) & (kpos >= 0)[None, :] & (kpos < t_len)[None, :]
        s_loc = jnp.einsum('bqkgd,bskd->bkgqs', qj, kj).astype(jnp.float32) * SCALE_B
        s_loc = jnp.where(valid, s_loc, -jnp.inf)
        s_ctx = jnp.einsum('bqkgd,bskd->bkgqs', qj, kc).astype(jnp.float32) * SCALE_B
        p = _sink_softmax(jnp.concatenate([s_loc, s_ctx], axis=-1), sink)
        o = (jnp.einsum('bkgqs,bskd->bqkgd', p[..., :3 * BLK_B], vj)
             + jnp.einsum('bkgqs,bskd->bqkgd', p[..., 3 * BLK_B:], vc))
        return o.reshape(bsz, BLK_B, QW_B).astype(q.dtype)

    xs = (jnp.moveaxis(qb, 1, 0), jnp.moveaxis(kband, 1, 0), jnp.moveaxis(vband, 1, 0), jnp.arange(nb))
    o = lax.map(block, xs)
    return jnp.moveaxis(o, 0, 1).reshape(bsz, t_len, QW_B)


def _hgrn2_scan_inputs(u, w_scan, lb):
    bsz, t_len, _ = u.shape
    z = (u @ w_scan).astype(jnp.float32).reshape(bsz, t_len, 3, H_A, DK_A)
    v = z[:, :, 0]
    zf = z[:, :, 1:]
    lbf = lb.astype(jnp.float32).reshape(2, H_A, DK_A)
    logf = jnp.logaddexp(jnp.log(lbf), jnp.log1p(-lbf) + jax.nn.log_sigmoid(zf))
    k = (1.0 - lbf) * jax.nn.sigmoid(-zf)
    return [(k[:, :, d], v, logf[:, :, d]) for d in range(2)]


def _hgrn2_mixer(uc, ul, w_in, lb, onorm, w_out, emit_ctx):
    w_read, w_scan = w_in[:, :2 * D_A], w_in[:, 2 * D_A:]

    def readout_inputs(u):
        qg = u @ w_read
        q = jax.nn.silu(qg[..., :D_A].astype(jnp.float32)).reshape(u.shape[0], u.shape[1], H_A, DK_A)
        return q, qg[..., D_A:]

    def readout(o, g, dtype):
        o = _rmsnorm(o, onorm.reshape(H_A, DK_A)).reshape(o.shape[0], o.shape[1], D_A)
        return (o * jax.nn.silu(g.astype(jnp.float32))).astype(dtype) @ w_out

    s0 = jnp.zeros((ul.shape[0], H_A, DK_A, DK_A), jnp.float32)
    ql, gl = readout_inputs(ul)
    lat_dirs = [t + (ql,) for t in _hgrn2_scan_inputs(ul, w_scan, lb)]
    ctx_dirs = _hgrn2_scan_inputs(uc, w_scan, lb)
    if emit_ctx:
        qc, gc = readout_inputs(uc)
        ctx_dirs = [t + (qc,) for t in ctx_dirs]
    oc, ol = _bidir_scan(_glr_scan, s0, ctx_dirs, lat_dirs)
    yc = readout(oc, gc, uc.dtype) if emit_ctx else None
    return yc, readout(ol, gl, ul.dtype)


def _swa_mixer(uc, ul, w_qkv, sink, w_out, ang_row, ang_col, emit_ctx):
    bsz, t_len, _ = ul.shape
    l_len = uc.shape[1]
    ql, kl, vl = jnp.split(ul @ w_qkv, [QW_B, QW_B + KW_B], axis=-1)
    ql = _rope_2d(ql.reshape(bsz, t_len, HQ_B, DH_B), ang_row, ang_col)
    kl = _rope_2d(kl.reshape(bsz, t_len, HKV_B, DH_B), ang_row, ang_col)
    vl = vl.reshape(bsz, t_len, HKV_B, DH_B)
    kc, vc = jnp.split(uc @ w_qkv[:, QW_B:], [KW_B], axis=-1)
    kc = kc.reshape(bsz, l_len, HKV_B, DH_B)
    vc = vc.reshape(bsz, l_len, HKV_B, DH_B)
    sink_f = sink.astype(jnp.float32).reshape(HKV_B, GRP_B)
    yl = _banded_attention(ql, kl, vl, kc, vc, sink_f) @ w_out
    yc = None
    if emit_ctx:
        qc = (uc @ w_qkv[:, :QW_B]).reshape(bsz, l_len, HKV_B, GRP_B, DH_B)
        s = jnp.einsum('bqkgd,bskd->bkgqs', qc, kc).astype(jnp.float32) * SCALE_B
        p = _sink_softmax(s, sink_f)
        yc = jnp.einsum('bkgqs,bskd->bqkgd', p, vc).reshape(bsz, l_len, QW_B).astype(uc.dtype) @ w_out
    return yc, yl


def _ssd_mixer(uc, ul, w_in, conv_w, conv_b, dt_bias, a_log, d_skip, onorm, w_out, emit_ctx):
    a_neg = -jnp.exp(a_log.astype(jnp.float32))
    d_f = d_skip.astype(jnp.float32).reshape(N_GROUPS_C, HPG_C, 1)

    def inputs(u):
        bsz, t_len, _ = u.shape
        z, xbc, dt_raw = jnp.split(u @ w_in, [D_INNER_C, D_INNER_C + CONV_CH_C], axis=-1)
        xbc = jax.nn.silu(_dwconv_centred(xbc, conv_w, conv_b)).astype(jnp.float32)
        xs, bm, cm = jnp.split(xbc, [D_INNER_C, D_INNER_C + GN_C], axis=-1)
        xs = xs.reshape(bsz, t_len, N_GROUPS_C, HPG_C, P_C)
        bm = bm.reshape(bsz, t_len, N_GROUPS_C, N_STATE_C)
        cm = cm.reshape(bsz, t_len, N_GROUPS_C, N_STATE_C)
        dt = jax.nn.softplus(dt_raw.astype(jnp.float32).reshape(bsz, t_len, 2, H_C) + dt_bias.astype(jnp.float32))
        la = (dt * a_neg).reshape(bsz, t_len, 2, N_GROUPS_C, HPG_C)
        dt = dt.reshape(bsz, t_len, 2, N_GROUPS_C, HPG_C)
        return z, xs, bm, cm, dt, la

    def readout(y, xs, z, dtype):
        bsz, t_len = y.shape[:2]
        y = (y + d_f * xs).reshape(bsz, t_len, D_INNER_C) * jax.nn.silu(z.astype(jnp.float32))
        y = _rmsnorm(y.reshape(bsz, t_len, N_GROUPS_C, -1), onorm.reshape(N_GROUPS_C, -1))
        return y.reshape(bsz, t_len, D_INNER_C).astype(dtype) @ w_out

    zl, xl, bl, cl, dtl, lal = inputs(ul)
    zc, xc, bc, cc, dtc, lac = inputs(uc)
    lat_dirs = [(xl, dtl[:, :, d], lal[:, :, d], bl, cl) for d in range(2)]
    ctx_dirs = [(xc, dtc[:, :, d], lac[:, :, d], bc) + ((cc,) if emit_ctx else ()) for d in range(2)]
    h0 = jnp.zeros((ul.shape[0], N_GROUPS_C, HPG_C, P_C, N_STATE_C), jnp.float32)
    yc, yl = _bidir_scan(_ssd_scan, h0, ctx_dirs, lat_dirs)
    out_c = readout(yc, xc, zc, uc.dtype) if emit_ctx else None
    return out_c, readout(yl, xl, zl, ul.dtype)


def _residual_update(h, y, m, g, w1, w2):
    h = h + m[2] * _rmsnorm(y, g[1])
    u = _modulate(h, g[2], m[3], m[4])
    return h + m[5] * _rmsnorm(_mlp(u, w1, w2), g[3])


def setup_inputs(seed: int = 0) -> dict:
    key = jax.random.key(seed)
    ks = jax.random.split(key, 24)
    D = D_MODEL

    def nrm(k, shape, scale):
        return jax.random.normal(k, shape, jnp.float32) * scale

    dt0 = jnp.exp(jax.random.uniform(ks[19], (N_C, 2, H_C), jnp.float32,
                                     minval=math.log(DT_MIN), maxval=math.log(DT_MAX)))
    return {
        'x': nrm(ks[0], (BATCH, SEQ, D), 1.0),
        'c': nrm(ks[1], (BATCH, D), 1.0),
        'ctx': nrm(ks[2], (BATCH, CTX_LEN, D), 1.0),
        'c_ctx': nrm(ks[3], (D,), 1.0),
        'ada_w': nrm(ks[4], (DEPTH, D, 6 * D), D ** -0.5),
        'ada_b': nrm(ks[5], (DEPTH, 6 * D), 0.01),
        'norm_g': 1.0 + nrm(ks[6], (DEPTH, 4, D), 0.02),
        'mlp_w1': nrm(ks[7], (DEPTH, D, D_FF), D ** -0.5),
        'mlp_w2': nrm(ks[8], (DEPTH, D_FF, D), D_FF ** -0.5),
        'a_w_in': nrm(ks[9], (N_A, D, 5 * D_A), D ** -0.5),
        'a_lb_logits': nrm(ks[10], (DEPTH, 2, D_A), 0.1),
        'a_onorm': 1.0 + nrm(ks[11], (N_A, D_A), 0.02),
        'a_w_out': nrm(ks[12], (N_A, D_A, D), D_A ** -0.5),
        'b_w_qkv': nrm(ks[13], (N_B, D, QW_B + 2 * KW_B), D ** -0.5),
        'b_sink': nrm(ks[14], (N_B, HQ_B), 0.5),
        'b_w_out': nrm(ks[15], (N_B, QW_B, D), QW_B ** -0.5),
        'c_w_in': nrm(ks[16], (N_C, D, IN_C), D ** -0.5),
        'c_conv_w': nrm(ks[17], (N_C, CONV_C, CONV_CH_C), CONV_C ** -0.5),
        'c_conv_b': nrm(ks[18], (N_C, CONV_CH_C), 0.01),
        'c_dt_bias': dt0 + jnp.log(-jnp.expm1(-dt0)),
        'c_a_log': jnp.log(jax.random.uniform(ks[20], (N_C, 2, H_C), jnp.float32, minval=1.0, maxval=16.0)),
        'c_d': 1.0 + nrm(ks[21], (N_C, H_C), 0.1),
        'c_onorm': 1.0 + nrm(ks[22], (N_C, D_INNER_C), 0.02),
        'c_w_out': nrm(ks[23], (N_C, D_INNER_C, D), D_INNER_C ** -0.5),
    }


def reference(x, c, ctx, c_ctx, ada_w, ada_b, norm_g, mlp_w1, mlp_w2, a_w_in, a_lb_logits, a_onorm,
              a_w_out, b_w_qkv, b_sink, b_w_out, c_w_in, c_conv_w, c_conv_b, c_dt_bias, c_a_log, c_d,
              c_onorm, c_w_out):
    t_len = x.shape[1]
    rows = t_len // GRID_W
    inv_freq = ROPE_BASE ** (-jnp.arange(ROPE_PAIRS, dtype=jnp.float32) / ROPE_PAIRS)
    row = jnp.repeat(jnp.arange(rows, dtype=jnp.float32), GRID_W)
    col = jnp.tile(jnp.arange(GRID_W, dtype=jnp.float32), rows)
    ang_row = row[:, None] * inv_freq
    ang_col = col[:, None] * inv_freq

    p_lb = jax.nn.softmax(a_lb_logits.astype(jnp.float32), axis=0)
    lower_bounds = jnp.cumsum(p_lb, axis=0) - p_lb

    s_lat = jax.nn.silu(c)
    s_ctx = jax.nn.silu(c_ctx)
    hl, hc = x, ctx
    for i in range(DEPTH):
        emit_ctx = i < DEPTH - 1
        m_l = jnp.split((s_lat @ ada_w[i] + ada_b[i])[:, None, :], 6, axis=-1)
        m_c = jnp.split(s_ctx @ ada_w[i] + ada_b[i], 6, axis=-1)
        ul = _modulate(hl, norm_g[i, 0], m_l[0], m_l[1])
        uc = _modulate(hc, norm_g[i, 0], m_c[0], m_c[1])
        kind, j = i % N_MIXERS, i // N_MIXERS
        if kind == 0:
            yc, yl = _hgrn2_mixer(uc, ul, a_w_in[j], lower_bounds[i], a_onorm[j], a_w_out[j], emit_ctx)
        elif kind == 1:
            yc, yl = _swa_mixer(uc, ul, b_w_qkv[j], b_sink[j], b_w_out[j], ang_row, ang_col, emit_ctx)
        else:
            yc, yl = _ssd_mixer(uc, ul, c_w_in[j], c_conv_w[j], c_conv_b[j], c_dt_bias[j], c_a_log[j],
                                c_d[j], c_onorm[j], c_w_out[j], emit_ctx)
        hl = _residual_update(hl, yl, m_l, norm_g[i], mlp_w1[i], mlp_w2[i])
        if emit_ctx:
            hc = _residual_update(hc, yc, m_c, norm_g[i], mlp_w1[i], mlp_w2[i])
    return hl
```

```python
import functools
import math

import jax
import jax.numpy as jnp
from jax import lax
from jax.experimental import pallas as pl
from jax.experimental.pallas import tpu as pltpu

F32 = jnp.float32
BF16 = jnp.bfloat16

EPS = 1e-6
GRID_W = 64
ROPE_BASE = 10000.0
N_MIXERS = 3

DK_A = 128
BAND_A = 16
HEADS_PER_BLOCK_A = 4

DH_B = 128
GRP_B = 4
WINDOW_B = 128
BLK_B = 128
SCALE_B = DH_B ** -0.5
ROPE_PAIRS = DH_B // 4

P_C = 64
N_GROUPS_C = 8
N_STATE_C = 128
CONV_C = 5
CHUNK_C = 128

VMEM_LIMIT = 48 * 1024 * 1024


def _params(*sem):
    return pltpu.CompilerParams(dimension_semantics=sem, vmem_limit_bytes=VMEM_LIMIT)


def _rms(x):
    return x * lax.rsqrt(jnp.mean(x * x, axis=-1, keepdims=True) + EPS)


def _silu(x):
    return x * jax.nn.sigmoid(x)


def _dot(a, b):
    return jnp.dot(a, b, preferred_element_type=F32)


def _dot_nt(a, b):
    return lax.dot_general(a, b, (((1,), (1,)), ((), ())), preferred_element_type=F32)


def _dot_tn(a, b):
    return lax.dot_general(a, b, (((0,), (0,)), ((), ())), preferred_element_type=F32)


def _ada_kernel(c_ref, w_ref, b_ref, o_ref):
    s = _silu(c_ref[...])
    o_ref[...] = jnp.dot(s, w_ref[...], precision=lax.Precision.HIGHEST,
                         preferred_element_type=F32) + b_ref[...]


def _ada(c_rows, ada_w, ada_b, tn=768):
    depth, d, n = ada_w.shape
    rows = c_rows.shape[0]
    return pl.pallas_call(
        _ada_kernel,
        grid=(depth, n // tn),
        in_specs=[pl.BlockSpec((rows, d), lambda l, j: (0, 0)),
                  pl.BlockSpec((None, d, tn), lambda l, j: (l, 0, j)),
                  pl.BlockSpec((None, 1, tn), lambda l, j: (l, 0, j))],
        out_specs=pl.BlockSpec((None, rows, tn), lambda l, j: (l, 0, j)),
        out_shape=jax.ShapeDtypeStruct((depth, rows, n), F32),
        compiler_params=_params("parallel", "parallel"),
        name="ada_modulation",
    )(c_rows, ada_w, ada_b.reshape(depth, 1, n))


def _modlinear_kernel(h_ref, mod_ref, g_ref, w_ref, o_ref, u_ref, *, g_row, shift_row, scale_row):
    @pl.when(pl.program_id(1) == 0)
    def _():
        y = _rms(h_ref[...]) * g_ref[g_row:g_row + 1, :]
        u = y * (1.0 + mod_ref[scale_row:scale_row + 1, :]) + mod_ref[shift_row:shift_row + 1, :]
        u_ref[...] = u.astype(BF16)

    o_ref[...] = _dot(u_ref[...], w_ref[...]).astype(o_ref.dtype)


def _modlinear(h, mod, g4, w, *, g_row, shift_row, scale_row, rows_per_mod, tn, tm=1024):
    m, d = h.shape
    n = w.shape[1]
    tm = min(tm, m, rows_per_mod)
    tiles_per_mod = rows_per_mod // tm
    kern = functools.partial(_modlinear_kernel, g_row=g_row, shift_row=shift_row, scale_row=scale_row)
    return pl.pallas_call(
        kern,
        grid=(m // tm, n // tn),
        in_specs=[pl.BlockSpec((tm, d), lambda i, j: (i, 0)),
                  pl.BlockSpec((None, 6, d), lambda i, j: (i // tiles_per_mod, 0, 0)),
                  pl.BlockSpec((4, d), lambda i, j: (0, 0)),
                  pl.BlockSpec((d, tn), lambda i, j: (0, j))],
        out_specs=pl.BlockSpec((tm, tn), lambda i, j: (i, j)),
        out_shape=jax.ShapeDtypeStruct((m, n), F32),
        scratch_shapes=[pltpu.VMEM((tm, d), BF16)],
        compiler_params=_params("parallel", "arbitrary"),
        name="modulate_in_proj",
    )(h, mod, g4, w)


def _linear_resnorm_kernel(a_ref, w_ref, h_ref, mod_ref, g_ref, o_ref, acc_ref, *, g_row, gate_row):
    k = pl.program_id(1)

    @pl.when(k == 0)
    def _():
        acc_ref[...] = jnp.zeros_like(acc_ref)

    acc_ref[...] += _dot(a_ref[...], w_ref[...])

    @pl.when(k == pl.num_programs(1) - 1)
    def _():
        y = _rms(acc_ref[...]) * g_ref[g_row:g_row + 1, :]
        o_ref[...] = h_ref[...] + mod_ref[gate_row:gate_row + 1, :] * y


def _linear_resnorm(a, w, h, mod, g4, *, g_row, gate_row, rows_per_mod, tm=512, tk=512):
    m, kdim = a.shape
    d = w.shape[1]
    tm = min(tm, m, rows_per_mod)
    tiles_per_mod = rows_per_mod // tm
    kern = functools.partial(_linear_resnorm_kernel, g_row=g_row, gate_row=gate_row)
    return pl.pallas_call(
        kern,
        grid=(m // tm, kdim // tk),
        in_specs=[pl.BlockSpec((tm, tk), lambda i, k: (i, k)),
                  pl.BlockSpec((tk, d), lambda i, k: (k, 0)),
                  pl.BlockSpec((tm, d), lambda i, k: (i, 0)),
                  pl.BlockSpec((None, 6, d), lambda i, k: (i // tiles_per_mod, 0, 0)),
                  pl.BlockSpec((4, d), lambda i, k: (0, 0))],
        out_specs=pl.BlockSpec((tm, d), lambda i, k: (i, 0)),
        out_shape=jax.ShapeDtypeStruct((m, d), F32),
        scratch_shapes=[pltpu.VMEM((tm, d), F32)],
        compiler_params=_params("parallel", "arbitrary"),
        name="out_proj_residual",
    )(a, w, h, mod, g4)


def _mlp_kernel(h_ref, mod_ref, g_ref, w1_ref, w2_ref, o_ref, u_ref, acc_ref):
    j = pl.program_id(1)

    @pl.when(j == 0)
    def _():
        y = _rms(h_ref[...]) * g_ref[2:3, :]
        u_ref[...] = (y * (1.0 + mod_ref[4:5, :]) + mod_ref[3:4, :]).astype(BF16)
        acc_ref[...] = jnp.zeros_like(acc_ref)

    a = jnp.maximum(_dot(u_ref[...], w1_ref[...]), 0.0)
    acc_ref[...] += _dot((a * a).astype(BF16), w2_ref[...])

    @pl.when(j == pl.num_programs(1) - 1)
    def _():
        y = _rms(acc_ref[...]) * g_ref[3:4, :]
        o_ref[...] = h_ref[...] + mod_ref[5:6, :] * y


def _mlp(h, mod, g4, w1, w2, *, rows_per_mod, tm=512, tf=512):
    m, d = h.shape
    ff = w1.shape[1]
    tm = min(tm, m, rows_per_mod)
    tiles_per_mod = rows_per_mod // tm
    return pl.pallas_call(
        _mlp_kernel,
        grid=(m // tm, ff // tf),
        in_specs=[pl.BlockSpec((tm, d), lambda i, j: (i, 0)),
                  pl.BlockSpec((None, 6, d), lambda i, j: (i // tiles_per_mod, 0, 0)),
                  pl.BlockSpec((4, d), lambda i, j: (0, 0)),
                  pl.BlockSpec((d, tf), lambda i, j: (0, j)),
                  pl.BlockSpec((tf, d), lambda i, j: (j, 0))],
        out_specs=pl.BlockSpec((tm, d), lambda i, j: (i, 0)),
        out_shape=jax.ShapeDtypeStruct((m, d), F32),
        scratch_shapes=[pltpu.VMEM((tm, d), BF16), pltpu.VMEM((tm, d), F32)],
        compiler_params=_params("parallel", "arbitrary"),
        name="mlp_residual",
    )(h, mod, g4, w1, w2)


def _hgrn2_scan_kernel(zq_ref, zv_ref, zf_ref, lb_ref, s0_ref, o_ref, sfin_ref,
                       st_ref, q_s, k_s, v_s, lf_s, g_s, *, reverse, tb, hpb):
    t = pl.program_id(2)
    pad = BAND_A
    width = hpb * DK_A

    @pl.when(t == 0)
    def _():
        st_ref[...] = s0_ref[...]

    zero_pad = jnp.zeros((pad, width), F32)
    for ref in (k_s, v_s, lf_s):
        ref[0:pad, :] = zero_pad
        ref[pad + tb:pad + tb + pad, :] = zero_pad

    lb = lb_ref[...]
    zf = zf_ref[...]
    q_s[...] = _silu(zq_ref[...])
    k_s[pad:pad + tb, :] = (1.0 - lb) * jax.nn.sigmoid(-zf)
    lf_s[pad:pad + tb, :] = jnp.logaddexp(jnp.log(lb), jnp.log1p(-lb) + jax.nn.log_sigmoid(zf))
    v_s[pad:pad + tb, :] = zv_ref[...]

    rb = 64
    row_in_chunk = lax.broadcasted_iota(jnp.int32, (rb, 1), 0) % BAND_A
    dist = (BAND_A - 1 - row_in_chunk) if reverse else row_in_chunk
    sgn = 1 if reverse else -1

    for h in range(hpb):
        hs = slice(h * DK_A, (h + 1) * DK_A)
        for r0 in range(0, tb, rb):
            qh = q_s[r0:r0 + rb, hs]
            w = jnp.zeros((rb, DK_A), F32)
            g = jnp.zeros((rb, DK_A), F32)
            o = jnp.zeros((rb, DK_A), F32)
            for dl in range(BAND_A):
                off = pad + r0 + sgn * dl
                e = qh * k_s[off:off + rb, hs]
                if dl:
                    e = e * jnp.exp(w)
                a = jnp.sum(e, axis=-1, keepdims=True)
                a = jnp.where(dist >= dl, a, 0.0)
                o = o + a * v_s[off:off + rb, hs]
                w = w + lf_s[off:off + rb, hs]
                g = jnp.where(dist == dl, w, g)
            o_ref[r0:r0 + rb, hs] = o
            g_s[r0:r0 + rb, hs] = g

    n_chunks = tb // BAND_A
    order = range(n_chunks - 1, -1, -1) if reverse else range(n_chunks)
    last = 0 if reverse else BAND_A - 1
    for c in order:
        rows = slice(c * BAND_A, (c + 1) * BAND_A)
        prow = slice(pad + c * BAND_A, pad + (c + 1) * BAND_A)
        for h in range(hpb):
            hs = slice(h * DK_A, (h + 1) * DK_A)
            gc = g_s[rows, hs]
            glast = gc[last:last + 1, :]
            qhat = (q_s[rows, hs] * jnp.exp(gc)).astype(BF16)
            khat = (k_s[prow, hs] * jnp.exp(glast - gc)).astype(BF16)
            st = st_ref[h]
            o_ref[rows, hs] += _dot_nt(qhat, st.astype(BF16))
            st_ref[h] = st * jnp.exp(glast) + _dot_tn(v_s[prow, hs].astype(BF16), khat)

    @pl.when(t == pl.num_programs(2) - 1)
    def _():
        sfin_ref[...] = st_ref[...]


def _hgrn2_scan(z, lb, s0, *, reverse, tb=128):
    bsz, t_len, _ = z.shape
    heads = s0.shape[1]
    hpb = HEADS_PER_BLOCK_A
    width = hpb * DK_A
    n_hb = heads // hpb
    nt = t_len // tb
    tt = (lambda t: nt - 1 - t) if reverse else (lambda t: t)
    f_base = (4 if reverse else 3) * n_hb
    kern = functools.partial(_hgrn2_scan_kernel, reverse=reverse, tb=tb, hpb=hpb)
    state_spec = pl.BlockSpec((None, hpb, DK_A, DK_A), lambda b, hb, t: (b, hb, 0, 0))
    return pl.pallas_call(
        kern,
        grid=(bsz, n_hb, nt),
        in_specs=[pl.BlockSpec((None, tb, width), lambda b, hb, t: (b, tt(t), hb)),
                  pl.BlockSpec((None, tb, width), lambda b, hb, t: (b, tt(t), 2 * n_hb + hb)),
                  pl.BlockSpec((None, tb, width), lambda b, hb, t: (b, tt(t), f_base + hb)),
                  pl.BlockSpec((1, width), lambda b, hb, t: (0, hb)),
                  state_spec],
        out_specs=[pl.BlockSpec((None, tb, width), lambda b, hb, t: (b, tt(t), hb)),
                   state_spec],
        out_shape=[jax.ShapeDtypeStruct((bsz, t_len, heads * DK_A), F32),
                   jax.ShapeDtypeStruct(s0.shape, F32)],
        scratch_shapes=[pltpu.VMEM((hpb, DK_A, DK_A), F32),
                        pltpu.VMEM((tb, width), F32),
                        pltpu.VMEM((tb + 2 * BAND_A, width), F32),
                        pltpu.VMEM((tb + 2 * BAND_A, width), F32),
                        pltpu.VMEM((tb + 2 * BAND_A, width), F32),
                        pltpu.VMEM((tb, width), F32)],
        compiler_params=_params("parallel", "parallel", "arbitrary"),
        name="hgrn2_scan_bwd" if reverse else "hgrn2_scan_fwd",
    )(z, z, z, lb, s0)


def _hgrn2_readout_kernel(of_ref, ob_ref, zg_ref, on_ref, a_ref, *, heads):
    for h in range(heads):
        hs = slice(h * DK_A, (h + 1) * DK_A)
        o = of_ref[:, hs] + ob_ref[:, hs]
        y = _rms(o) * on_ref[:, hs]
        a_ref[:, hs] = (y * _silu(zg_ref[:, hs])).astype(BF16)


def _hgrn2_readout(o_f, o_b, z2d, onorm, tm=512):
    m, d = o_f.shape
    tm = min(tm, m)
    kern = functools.partial(_hgrn2_readout_kernel, heads=d // DK_A)
    return pl.pallas_call(
        kern,
        grid=(m // tm,),
        in_specs=[pl.BlockSpec((tm, d), lambda i: (i, 0)),
                  pl.BlockSpec((tm, d), lambda i: (i, 0)),
                  pl.BlockSpec((tm, d), lambda i: (i, 1)),
                  pl.BlockSpec((1, d), lambda i: (0, 0))],
        out_specs=pl.BlockSpec((tm, d), lambda i: (i, 0)),
        out_shape=jax.ShapeDtypeStruct((m, d), BF16),
        compiler_params=_params("parallel"),
        name="hgrn2_readout",
    )(o_f, o_b, z2d, onorm)


def _rope_kernel(x_ref, cos_ref, sin_ref, o_ref, *, n_q, n_rot):
    cos = cos_ref[...]
    sin = sin_ref[...]
    lane = lax.broadcasted_iota(jnp.int32, cos.shape, 1)
    first_half = (lane % (DH_B // 2)) < ROPE_PAIRS
    for h in range(x_ref.shape[1] // DH_B):
        hs = slice(h * DH_B, (h + 1) * DH_B)
        x = x_ref[:, hs]
        if h < n_rot:
            partner = jnp.where(first_half, pltpu.roll(x, DH_B - ROPE_PAIRS, 1),
                                pltpu.roll(x, ROPE_PAIRS, 1))
            x = x * cos + partner * sin
        if h < n_q:
            x = x * SCALE_B
        o_ref[:, hs] = x.astype(BF16)


def _rope_cast(qkv, cos, sin, *, n_q, n_rot, tm=512):
    bsz, t_len, w = qkv.shape
    tm = min(tm, t_len)
    kern = functools.partial(_rope_kernel, n_q=n_q, n_rot=n_rot)
    return pl.pallas_call(
        kern,
        grid=(bsz, t_len // tm),
        in_specs=[pl.BlockSpec((None, tm, w), lambda b, i: (b, i, 0)),
                  pl.BlockSpec((tm, DH_B), lambda b, i: (i, 0)),
                  pl.BlockSpec((tm, DH_B), lambda b, i: (i, 0))],
        out_specs=pl.BlockSpec((None, tm, w), lambda b, i: (b, i, 0)),
        out_shape=jax.ShapeDtypeStruct((bsz, t_len, w), BF16),
        compiler_params=_params("parallel", "parallel"),
        name="rope_cast",
    )(qkv, cos, sin)


def _sink_column(sink_ref, kv, rows_per_head):
    return jnp.concatenate(
        [jnp.full((rows_per_head, 1), sink_ref[kv, g], F32) for g in range(GRP_B)], axis=0)


def _stack_heads(q):
    return jnp.concatenate([q[:, g * DH_B:(g + 1) * DH_B] for g in range(GRP_B)], axis=0)


def _swa_kernel(sink_ref, q_ref, kp_ref, kc_ref, kn_ref, vp_ref, vc_ref, vn_ref, kx_ref, vx_ref,
                o_ref):
    kv = pl.program_id(1)
    j = pl.program_id(2)
    nb = pl.num_programs(2)
    qs = _stack_heads(q_ref[...])
    kb = jnp.concatenate([kp_ref[...], kc_ref[...], kn_ref[...]], axis=0)
    vb = jnp.concatenate([vp_ref[...], vc_ref[...], vn_ref[...]], axis=0)
    s_loc = _dot_nt(qs, kb)
    r = lax.broadcasted_iota(jnp.int32, s_loc.shape, 0) % BLK_B
    c = lax.broadcasted_iota(jnp.int32, s_loc.shape, 1)
    rel = c - BLK_B - r
    valid = (jnp.abs(rel) <= WINDOW_B)
    valid = valid & ((c >= BLK_B) | (j > 0)) & ((c < 2 * BLK_B) | (j < nb - 1))
    s_loc = jnp.where(valid, s_loc, -jnp.inf)
    s_ctx = _dot_nt(qs, kx_ref[...])
    sink = _sink_column(sink_ref, kv, BLK_B)
    m = jnp.maximum(jnp.maximum(jnp.max(s_loc, axis=-1, keepdims=True),
                                jnp.max(s_ctx, axis=-1, keepdims=True)), sink)
    p_loc = jnp.exp(s_loc - m)
    p_ctx = jnp.exp(s_ctx - m)
    den = (jnp.sum(p_loc, axis=-1, keepdims=True) + jnp.sum(p_ctx, axis=-1, keepdims=True)
           + jnp.exp(sink - m))
    o = (_dot(p_loc.astype(BF16), vb) + _dot(p_ctx.astype(BF16), vx_ref[...])) / den
    for g in range(GRP_B):
        o_ref[:, g * DH_B:(g + 1) * DH_B] = o[g * BLK_B:(g + 1) * BLK_B].astype(BF16)


def _swa_attention(sink, qkv, qkv_ctx, *, n_q, n_kv):
    bsz, t_len, _ = qkv.shape
    l_len = qkv_ctx.shape[1]
    nb = t_len // BLK_B
    gw = GRP_B * DH_B
    k0, v0 = n_q, n_q + n_kv

    def band(col0, shift):
        return pl.BlockSpec(
            (None, BLK_B, DH_B),
            lambda b, kv, j: (b, jnp.clip(j + shift, 0, nb - 1), col0 + kv))

    return pl.pallas_call(
        _swa_kernel,
        grid=(bsz, n_kv, nb),
        in_specs=[pl.BlockSpec(memory_space=pltpu.SMEM),
                  pl.BlockSpec((None, BLK_B, gw), lambda b, kv, j: (b, j, kv)),
                  band(k0, -1), band(k0, 0), band(k0, 1),
                  band(v0, -1), band(v0, 0), band(v0, 1),
                  pl.BlockSpec((None, l_len, DH_B), lambda b, kv, j: (b, 0, k0 + kv)),
                  pl.BlockSpec((None, l_len, DH_B), lambda b, kv, j: (b, 0, v0 + kv))],
        out_specs=pl.BlockSpec((None, BLK_B, gw), lambda b, kv, j: (b, j, kv)),
        out_shape=jax.ShapeDtypeStruct((bsz, t_len, n_q * DH_B), BF16),
        compiler_params=_params("parallel", "parallel", "parallel"),
        name="swa_attention",
    )(sink, qkv, qkv, qkv, qkv, qkv, qkv, qkv, qkv_ctx, qkv_ctx)


def _ctx_attention_kernel(sink_ref, q_ref, kx_ref, vx_ref, o_ref):
    kv = pl.program_id(1)
    l_len = q_ref.shape[0]
    qs = _stack_heads(q_ref[...])
    s = _dot_nt(qs, kx_ref[...])
    sink = _sink_column(sink_ref, kv, l_len)
    m = jnp.maximum(jnp.max(s, axis=-1, keepdims=True), sink)
    p = jnp.exp(s - m)
    den = jnp.sum(p, axis=-1, keepdims=True) + jnp.exp(sink - m)
    o = _dot(p.astype(BF16), vx_ref[...]) / den
    for g in range(GRP_B):
        o_ref[:, g * DH_B:(g + 1) * DH_B] = o[g * l_len:(g + 1) * l_len].astype(BF16)


def _ctx_attention(sink, qkv_ctx, *, n_q, n_kv):
    bsz, l_len, _ = qkv_ctx.shape
    gw = GRP_B * DH_B
    return pl.pallas_call(
        _ctx_attention_kernel,
        grid=(bsz, n_kv),
        in_specs=[pl.BlockSpec(memory_space=pltpu.SMEM),
                  pl.BlockSpec((None, l_len, gw), lambda b, kv: (b, 0, kv)),
                  pl.BlockSpec((None, l_len, DH_B), lambda b, kv: (b, 0, n_q + kv)),
                  pl.BlockSpec((None, l_len, DH_B), lambda b, kv: (b, 0, n_q + n_kv + kv))],
        out_specs=pl.BlockSpec((None, l_len, gw), lambda b, kv: (b, 0, kv)),
        out_shape=jax.ShapeDtypeStruct((bsz, l_len, n_q * DH_B), BF16),
        compiler_params=_params("parallel", "parallel"),
        name="ctx_attention",
    )(sink, qkv_ctx, qkv_ctx, qkv_ctx)


def _conv_silu_kernel(prev_ref, cur_ref, next_ref, w_ref, b_ref, o_ref, ext_ref, *, tb):
    i = pl.program_id(1)
    halo = 8
    has_prev = (i > 0).astype(F32)
    has_next = (i < pl.num_programs(1) - 1).astype(F32)
    ext_ref[0:halo, :] = prev_ref[...] * has_prev
    ext_ref[halo:halo + tb, :] = cur_ref[...]
    ext_ref[halo + tb:halo + tb + halo, :] = next_ref[...] * has_next
    acc = jnp.zeros(o_ref.shape, F32) + b_ref[...]
    for tap in range(CONV_C):
        off = halo + tap - CONV_C // 2
        acc = acc + ext_ref[off:off + tb, :] * w_ref[tap:tap + 1, :]
    o_ref[...] = _silu(acc)


def _conv_silu(z, conv_w, conv_b, *, col0, tb=512, tc=512):
    bsz, t_len, _ = z.shape
    ch = conv_w.shape[1]
    tb = min(tb, t_len)
    halo = 8
    hb = tb // halo
    n_h = t_len // halo
    c0 = col0 // tc
    kern = functools.partial(_conv_silu_kernel, tb=tb)
    return pl.pallas_call(
        kern,
        grid=(bsz, t_len // tb, ch // tc),
        in_specs=[pl.BlockSpec((None, halo, tc), lambda b, i, c: (b, jnp.maximum(i * hb - 1, 0), c0 + c)),
                  pl.BlockSpec((None, tb, tc), lambda b, i, c: (b, i, c0 + c)),
                  pl.BlockSpec((None, halo, tc), lambda b, i, c: (b, jnp.minimum((i + 1) * hb, n_h - 1), c0 + c)),
                  pl.BlockSpec((CONV_C, tc), lambda b, i, c: (0, c)),
                  pl.BlockSpec((1, tc), lambda b, i, c: (0, c))],
        out_specs=pl.BlockSpec((None, tb, tc), lambda b, i, c: (b, i, c)),
        out_shape=jax.ShapeDtypeStruct((bsz, t_len, ch), F32),
        scratch_shapes=[pltpu.VMEM((tb + 2 * halo, tc), F32)],
        compiler_params=_params("parallel", "parallel", "parallel"),
        name="conv_silu",
    )(z, z, z, conv_w, conv_b)


def _ssd_scan_kernel(x_ref, b_ref, c_ref, dtr_ref, dtb_ref, an_ref, h0_ref, y_ref, hfin_ref,
                     st_ref, *, reverse, groups):
    t = pl.program_id(1)
    cl = CHUNK_C
    heads = groups * (x_ref.shape[1] // groups // P_C)
    hpg = heads // groups
    lane0 = heads if reverse else 0

    @pl.when(t == 0)
    def _():
        st_ref[...] = h0_ref[...]

    dt = jnp.logaddexp(dtr_ref[...] + dtb_ref[...], 0.0)
    la = dt * an_ref[...]
    ri = lax.broadcasted_iota(jnp.int32, (cl, cl), 0)
    ci = lax.broadcasted_iota(jnp.int32, (cl, cl), 1)
    causal = (ci >= ri) if reverse else (ci <= ri)
    cum = jnp.dot(causal.astype(F32), la, precision=lax.Precision.HIGHEST,
                  preferred_element_type=F32)
    cum_t = cum.T
    dt_t = dt.T
    last = 0 if reverse else cl - 1
    lane = lax.broadcasted_iota(jnp.int32, (cl, 2 * P_C), 1)
    lo = lane < P_C

    for g in range(groups):
        bm = b_ref[:, g * N_STATE_C:(g + 1) * N_STATE_C].astype(BF16)
        cm = c_ref[:, g * N_STATE_C:(g + 1) * N_STATE_C].astype(BF16)
        cb = _dot_nt(cm, bm)
        for pair in range(hpg // 2):
            h_a = g * hpg + 2 * pair
            xs = slice(h_a * P_C, (h_a + 2) * P_C)
            x_pair = x_ref[:, xs]
            x_bf = x_pair.astype(BF16)
            ys = []
            cols = []
            for h in (h_a, h_a + 1):
                ln = lane0 + h
                cum_col = cum[:, ln:ln + 1]
                seg = jnp.exp(jnp.where(causal, cum_col - cum_t[ln:ln + 1, :], -jnp.inf))
                mat = (cb * seg * dt_t[ln:ln + 1, :]).astype(BF16)
                ys.append(_dot(mat, x_bf))
                cols.append((cum_col, dt[:, ln:ln + 1]))
            (cum_a, dt_a), (cum_b, dt_b) = cols
            idx = g * (hpg // 2) + pair
            st = st_ref[idx]
            ecum = jnp.where(lo, jnp.exp(cum_a), jnp.exp(cum_b))
            y = jnp.where(lo, ys[0], ys[1]) + ecum * _dot(cm, st.astype(BF16))
            y_ref[:, xs] = y
            tot_a = cum_a[last:last + 1, :]
            tot_b = cum_b[last:last + 1, :]
            wgt = jnp.where(lo, jnp.exp(tot_a - cum_a) * dt_a, jnp.exp(tot_b - cum_b) * dt_b)
            etot = jnp.where(lo[0:1, :], jnp.exp(tot_a), jnp.exp(tot_b))
            st_ref[idx] = st * etot + _dot_tn(bm, (x_pair * wgt).astype(BF16))

    @pl.when(t == pl.num_programs(1) - 1)
    def _():
        hfin_ref[...] = st_ref[...]


def _ssd_scan(xbc, z, dt_bias, a_neg, h0, *, reverse, d_inner, groups, dt_col0):
    bsz, t_len, _ = xbc.shape
    n_pairs = h0.shape[1]
    nc = t_len // CHUNK_C
    gn = groups * N_STATE_C
    two_h = dt_bias.shape[1]
    tt = (lambda t: nc - 1 - t) if reverse else (lambda t: t)
    kern = functools.partial(_ssd_scan_kernel, reverse=reverse, groups=groups)
    state_spec = pl.BlockSpec((None, n_pairs, N_STATE_C, 2 * P_C), lambda b, t: (b, 0, 0, 0))
    return pl.pallas_call(
        kern,
        grid=(bsz, nc),
        in_specs=[pl.BlockSpec((None, CHUNK_C, d_inner), lambda b, t: (b, tt(t), 0)),
                  pl.BlockSpec((None, CHUNK_C, gn), lambda b, t: (b, tt(t), d_inner // gn)),
                  pl.BlockSpec((None, CHUNK_C, gn), lambda b, t: (b, tt(t), d_inner // gn + 1)),
                  pl.BlockSpec((None, CHUNK_C, two_h), lambda b, t: (b, tt(t), dt_col0 // two_h)),
                  pl.BlockSpec((1, two_h), lambda b, t: (0, 0)),
                  pl.BlockSpec((1, two_h), lambda b, t: (0, 0)),
                  state_spec],
        out_specs=[pl.BlockSpec((None, CHUNK_C, d_inner), lambda b, t: (b, tt(t), 0)),
                   state_spec],
        out_shape=[jax.ShapeDtypeStruct((bsz, t_len, d_inner), F32),
                   jax.ShapeDtypeStruct(h0.shape, F32)],
        scratch_shapes=[pltpu.VMEM(h0.shape[1:], F32)],
        compiler_params=_params("parallel", "arbitrary"),
        name="ssd_scan_bwd" if reverse else "ssd_scan_fwd",
    )(xbc, xbc, xbc, z, dt_bias, a_neg, h0)


def _ssd_readout_kernel(yf_ref, yb_ref, x_ref, z_ref, d_ref, on_ref, a_ref, *, groups):
    gw = a_ref.shape[1] // groups
    for g in range(groups):
        gs = slice(g * gw, (g + 1) * gw)
        y = (yf_ref[:, gs] + yb_ref[:, gs] + d_ref[:, gs] * x_ref[:, gs]) * _silu(z_ref[:, gs])
        a_ref[:, gs] = (_rms(y) * on_ref[:, gs]).astype(BF16)


def _ssd_readout(y_f, y_b, xbc2d, z2d, d_exp, onorm, *, groups, tm=256):
    m, d_inner = y_f.shape
    tm = min(tm, m)
    kern = functools.partial(_ssd_readout_kernel, groups=groups)
    row = lambda i: (i, 0)
    return pl.pallas_call(
        kern,
        grid=(m // tm,),
        in_specs=[pl.BlockSpec((tm, d_inner), row),
                  pl.BlockSpec((tm, d_inner), row),
                  pl.BlockSpec((tm, d_inner), row),
                  pl.BlockSpec((tm, d_inner), row),
                  pl.BlockSpec((1, d_inner), lambda i: (0, 0)),
                  pl.BlockSpec((1, d_inner), lambda i: (0, 0))],
        out_specs=pl.BlockSpec((tm, d_inner), row),
        out_shape=jax.ShapeDtypeStruct((m, d_inner), BF16),
        compiler_params=_params("parallel"),
        name="ssd_readout",
    )(y_f, y_b, xbc2d, z2d, d_exp, onorm)


def _in_proj(h, mod, g4, w, *, rows_per_mod, tn):
    tm = 1024 if tn <= 512 else 512
    return _modlinear(h, mod, g4, w, g_row=0, shift_row=0, scale_row=1,
                      rows_per_mod=rows_per_mod, tn=tn, tm=tm)


def _hgrn2_mixer(zc, zl, lb, onorm, emit_ctx):
    bsz, t_len, five_d = zl.shape
    d_a = five_d // 5
    heads = d_a // DK_A
    l_len = zc.shape[1]
    s0 = jnp.zeros((bsz, heads, DK_A, DK_A), F32)
    o_lat, o_ctx = [], []
    for d in range(2):
        lb_d = lb[d].reshape(1, d_a)
        oc, s_ctx = _hgrn2_scan(zc, lb_d, s0, reverse=bool(d))
        ol, _ = _hgrn2_scan(zl, lb_d, s_ctx, reverse=bool(d))
        o_ctx.append(oc)
        o_lat.append(ol)
    on = onorm.reshape(1, d_a)
    al = _hgrn2_readout(o_lat[0].reshape(bsz * t_len, d_a), o_lat[1].reshape(bsz * t_len, d_a),
                        zl.reshape(bsz * t_len, five_d), on)
    ac = None
    if emit_ctx:
        ac = _hgrn2_readout(o_ctx[0].reshape(bsz * l_len, d_a), o_ctx[1].reshape(bsz * l_len, d_a),
                            zc.reshape(bsz * l_len, five_d), on)
    return ac, al


def _rope_tables(t_len):
    rows = t_len // GRID_W
    inv_freq = ROPE_BASE ** (-jnp.arange(ROPE_PAIRS, dtype=F32) / ROPE_PAIRS)
    row = jnp.repeat(jnp.arange(rows, dtype=F32), GRID_W)
    col = jnp.tile(jnp.arange(GRID_W, dtype=F32), rows)
    ang_row = row[:, None] * inv_freq
    ang_col = col[:, None] * inv_freq
    cos = jnp.concatenate([jnp.cos(ang_row)] * 2 + [jnp.cos(ang_col)] * 2, axis=-1)
    sin = jnp.concatenate([-jnp.sin(ang_row), jnp.sin(ang_row),
                           -jnp.sin(ang_col), jnp.sin(ang_col)], axis=-1)
    return cos, sin


def _swa_mixer(zc, zl, sink, emit_ctx):
    bsz, t_len, w = zl.shape
    l_len = zc.shape[1]
    n_kv = sink.shape[0] // GRP_B
    n_q = sink.shape[0]
    cos, sin = _rope_tables(t_len)
    ql = _rope_cast(zl, cos, sin, n_q=n_q, n_rot=n_q + n_kv)
    ones = jnp.ones((l_len, DH_B), F32)
    qc = _rope_cast(zc, ones, ones, n_q=n_q, n_rot=0)
    sink2 = sink.astype(F32).reshape(n_kv, GRP_B)
    al = _swa_attention(sink2, ql, qc, n_q=n_q, n_kv=n_kv).reshape(bsz * t_len, n_q * DH_B)
    ac = None
    if emit_ctx:
        ac = _ctx_attention(sink2, qc, n_q=n_q, n_kv=n_kv).reshape(bsz * l_len, n_q * DH_B)
    return ac, al


def _ssd_mixer(zc, zl, conv_w, conv_b, dt_bias, a_log, d_skip, onorm, emit_ctx):
    bsz = zl.shape[0]
    d_inner = onorm.shape[0]
    heads = d_skip.shape[0]
    groups = N_GROUPS_C
    gn = groups * N_STATE_C
    a_neg = (-jnp.exp(a_log.astype(F32))).reshape(1, 2 * heads)
    dtb = dt_bias.astype(F32).reshape(1, 2 * heads)
    d_exp = jnp.repeat(d_skip.astype(F32), P_C).reshape(1, d_inner)
    on = onorm.reshape(1, d_inner)
    cb = conv_b.reshape(1, -1)
    xbc_c = _conv_silu(zc, conv_w, cb, col0=d_inner)
    xbc_l = _conv_silu(zl, conv_w, cb, col0=d_inner)
    h0 = jnp.zeros((bsz, heads // 2, N_STATE_C, 2 * P_C), F32)
    kw = dict(d_inner=d_inner, groups=groups, dt_col0=2 * d_inner + 2 * gn)
    y_lat, y_ctx = [], []
    for d in range(2):
        yc, h_ctx = _ssd_scan(xbc_c, zc, dtb, a_neg, h0, reverse=bool(d), **kw)
        yl, _ = _ssd_scan(xbc_l, zl, dtb, a_neg, h_ctx, reverse=bool(d), **kw)
        y_ctx.append(yc)
        y_lat.append(yl)

    def readout(ys, xbc, z):
        m = z.shape[0] * z.shape[1]
        return _ssd_readout(ys[0].reshape(m, d_inner), ys[1].reshape(m, d_inner),
                            xbc.reshape(m, -1), z.reshape(m, -1), d_exp, on, groups=groups)

    ac = readout(y_ctx, xbc_c, zc) if emit_ctx else None
    return ac, readout(y_lat, xbc_l, zl)


def kernel(x, c, ctx, c_ctx, ada_w, ada_b, norm_g, mlp_w1, mlp_w2, a_w_in, a_lb_logits, a_onorm,
           a_w_out, b_w_qkv, b_sink, b_w_out, c_w_in, c_conv_w, c_conv_b, c_dt_bias, c_a_log, c_d,
           c_onorm, c_w_out):
    bsz, t_len, d = x.shape
    l_len = ctx.shape[1]
    depth = ada_w.shape[0]

    p_lb = jax.nn.softmax(a_lb_logits.astype(F32), axis=0)
    lower_bounds = jnp.cumsum(p_lb, axis=0) - p_lb

    c_rows = jnp.concatenate([c, c_ctx[None, :], jnp.zeros((8 - bsz - 1, d), F32)], axis=0)
    mods = _ada(c_rows, ada_w, ada_b).reshape(depth, 8, 6, d)

    hl = x.reshape(bsz * t_len, d)
    hc = ctx.reshape(bsz * l_len, d)
    rows_l, rows_c = t_len, bsz * l_len
    for i in range(depth):
        emit_ctx = i < depth - 1
        m_l = mods[i, :bsz]
        m_c = mods[i, bsz:bsz + 1]
        g4 = norm_g[i]
        kind, j = i % N_MIXERS, i // N_MIXERS
        if kind == 0:
            w_in, w_out, tn = a_w_in[j].astype(BF16), a_w_out[j].astype(BF16), 512
        elif kind == 1:
            w_in, w_out, tn = b_w_qkv[j].astype(BF16), b_w_out[j].astype(BF16), 512
        else:
            w_in, w_out, tn = c_w_in[j].astype(BF16), c_w_out[j].astype(BF16), 1152
        zl = _in_proj(hl, m_l, g4, w_in, rows_per_mod=rows_l, tn=tn).reshape(bsz, t_len, -1)
        zc = _in_proj(hc, m_c, g4, w_in, rows_per_mod=rows_c, tn=tn).reshape(bsz, l_len, -1)
        if kind == 0:
            ac, al = _hgrn2_mixer(zc, zl, lower_bounds[i], a_onorm[j], emit_ctx)
        elif kind == 1:
            ac, al = _swa_mixer(zc, zl, b_sink[j], emit_ctx)
        else:
            ac, al = _ssd_mixer(zc, zl, c_conv_w[j], c_conv_b[j], c_dt_bias[j], c_a_log[j], c_d[j],
                                c_onorm[j], emit_ctx)
        w1 = mlp_w1[i].astype(BF16)
        w2 = mlp_w2[i].astype(BF16)
        hl = _linear_resnorm(al, w_out, hl, m_l, g4, g_row=1, gate_row=2, rows_per_mod=rows_l)
        hl = _mlp(hl, m_l, g4, w1, w2, rows_per_mod=rows_l)
        if emit_ctx:
            hc = _linear_resnorm(ac, w_out, hc, m_c, g4, g_row=1, gate_row=2, rows_per_mod=rows_c)
            hc = _mlp(hc, m_c, g4, w1, w2, rows_per_mod=rows_c)
    return hl.reshape(bsz, t_len, d)
```

```python
import functools
import math

import jax
import jax.numpy as jnp
from jax import lax
from jax.experimental import pallas as pl
from jax.experimental.pallas import tpu as pltpu

F32 = jnp.float32
BF16 = jnp.bfloat16

EPS = 1e-6
GRID_W = 64
ROPE_BASE = 10000.0
N_MIXERS = 3

DK_A = 128
BLOCK_A = 128
HEADS_PER_BLOCK_A = 4
SUBLANES = 8

DH_B = 128
GRP_B = 4
WINDOW_B = 128
BLK_B = 128
SCALE_B = DH_B ** -0.5
ROPE_PAIRS = DH_B // 4

P_C = 64
N_GROUPS_C = 8
N_STATE_C = 128
CONV_C = 5
CHUNK_C = 128

VMEM_LIMIT = 48 * 1024 * 1024


def _params(*sem):
    return pltpu.CompilerParams(dimension_semantics=sem, vmem_limit_bytes=VMEM_LIMIT)


def _rms(x):
    return x * lax.rsqrt(jnp.mean(x * x, axis=-1, keepdims=True) + EPS)


def _silu(x):
    return x * jax.nn.sigmoid(x)


def _dot(a, b):
    return jnp.dot(a, b, preferred_element_type=F32)


def _dot_nt(a, b):
    return lax.dot_general(a, b, (((1,), (1,)), ((), ())), preferred_element_type=F32)


def _dot_tn(a, b):
    return lax.dot_general(a, b, (((0,), (0,)), ((), ())), preferred_element_type=F32)


def _ada_kernel(c_ref, w_ref, b_ref, o_ref):
    s = _silu(c_ref[...])
    o_ref[...] = jnp.dot(s, w_ref[...], precision=lax.Precision.HIGHEST,
                         preferred_element_type=F32) + b_ref[...]


def _ada(c_rows, ada_w, ada_b, tn=768):
    depth, d, n = ada_w.shape
    rows = c_rows.shape[0]
    return pl.pallas_call(
        _ada_kernel,
        grid=(depth, n // tn),
        in_specs=[pl.BlockSpec((rows, d), lambda l, j: (0, 0)),
                  pl.BlockSpec((None, d, tn), lambda l, j: (l, 0, j)),
                  pl.BlockSpec((None, 1, tn), lambda l, j: (l, 0, j))],
        out_specs=pl.BlockSpec((None, rows, tn), lambda l, j: (l, 0, j)),
        out_shape=jax.ShapeDtypeStruct((depth, rows, n), F32),
        compiler_params=_params("parallel", "parallel"),
        name="ada_modulation",
    )(c_rows, ada_w, ada_b.reshape(depth, 1, n))


def _modlinear_kernel(h_ref, mod_ref, g_ref, w_ref, o_ref, u_ref, *, g_row, shift_row, scale_row):
    @pl.when(pl.program_id(1) == 0)
    def _():
        y = _rms(h_ref[...]) * g_ref[g_row:g_row + 1, :]
        u = y * (1.0 + mod_ref[scale_row:scale_row + 1, :]) + mod_ref[shift_row:shift_row + 1, :]
        u_ref[...] = u.astype(BF16)

    o_ref[...] = _dot(u_ref[...], w_ref[...]).astype(o_ref.dtype)


def _modlinear(h, mod, g4, w, *, g_row, shift_row, scale_row, rows_per_mod, tn, tm=1024):
    m, d = h.shape
    n = w.shape[1]
    tm = min(tm, m, rows_per_mod)
    tiles_per_mod = rows_per_mod // tm
    kern = functools.partial(_modlinear_kernel, g_row=g_row, shift_row=shift_row, scale_row=scale_row)
    return pl.pallas_call(
        kern,
        grid=(m // tm, n // tn),
        in_specs=[pl.BlockSpec((tm, d), lambda i, j: (i, 0)),
                  pl.BlockSpec((None, 6, d), lambda i, j: (i // tiles_per_mod, 0, 0)),
                  pl.BlockSpec((4, d), lambda i, j: (0, 0)),
                  pl.BlockSpec((d, tn), lambda i, j: (0, j))],
        out_specs=pl.BlockSpec((tm, tn), lambda i, j: (i, j)),
        out_shape=jax.ShapeDtypeStruct((m, n), F32),
        scratch_shapes=[pltpu.VMEM((tm, d), BF16)],
        compiler_params=_params("parallel", "arbitrary"),
        name="modulate_in_proj",
    )(h, mod, g4, w)


def _linear_resnorm_kernel(a_ref, w_ref, h_ref, mod_ref, g_ref, o_ref, acc_ref, *, g_row, gate_row):
    k = pl.program_id(1)

    @pl.when(k == 0)
    def _():
        acc_ref[...] = jnp.zeros_like(acc_ref)

    acc_ref[...] += _dot(a_ref[...], w_ref[...])

    @pl.when(k == pl.num_programs(1) - 1)
    def _():
        y = _rms(acc_ref[...]) * g_ref[g_row:g_row + 1, :]
        o_ref[...] = h_ref[...] + mod_ref[gate_row:gate_row + 1, :] * y


def _linear_resnorm(a, w, h, mod, g4, *, g_row, gate_row, rows_per_mod, tm=512, tk=512):
    m, kdim = a.shape
    d = w.shape[1]
    tm = min(tm, m, rows_per_mod)
    tiles_per_mod = rows_per_mod // tm
    kern = functools.partial(_linear_resnorm_kernel, g_row=g_row, gate_row=gate_row)
    return pl.pallas_call(
        kern,
        grid=(m // tm, kdim // tk),
        in_specs=[pl.BlockSpec((tm, tk), lambda i, k: (i, k)),
                  pl.BlockSpec((tk, d), lambda i, k: (k, 0)),
                  pl.BlockSpec((tm, d), lambda i, k: (i, 0)),
                  pl.BlockSpec((None, 6, d), lambda i, k: (i // tiles_per_mod, 0, 0)),
                  pl.BlockSpec((4, d), lambda i, k: (0, 0))],
        out_specs=pl.BlockSpec((tm, d), lambda i, k: (i, 0)),
        out_shape=jax.ShapeDtypeStruct((m, d), F32),
        scratch_shapes=[pltpu.VMEM((tm, d), F32)],
        compiler_params=_params("parallel", "arbitrary"),
        name="out_proj_residual",
    )(a, w, h, mod, g4)


def _mlp_kernel(h_ref, mod_ref, g_ref, w1_ref, w2_ref, o_ref, u_ref, acc_ref):
    j = pl.program_id(1)

    @pl.when(j == 0)
    def _():
        y = _rms(h_ref[...]) * g_ref[2:3, :]
        u_ref[...] = (y * (1.0 + mod_ref[4:5, :]) + mod_ref[3:4, :]).astype(BF16)
        acc_ref[...] = jnp.zeros_like(acc_ref)

    a = jnp.maximum(_dot(u_ref[...], w1_ref[...]), 0.0)
    acc_ref[...] += _dot((a * a).astype(BF16), w2_ref[...])

    @pl.when(j == pl.num_programs(1) - 1)
    def _():
        y = _rms(acc_ref[...]) * g_ref[3:4, :]
        o_ref[...] = h_ref[...] + mod_ref[5:6, :] * y


def _mlp(h, mod, g4, w1, w2, *, rows_per_mod, tm=512, tf=512):
    m, d = h.shape
    ff = w1.shape[1]
    tm = min(tm, m, rows_per_mod)
    tiles_per_mod = rows_per_mod // tm
    return pl.pallas_call(
        _mlp_kernel,
        grid=(m // tm, ff // tf),
        in_specs=[pl.BlockSpec((tm, d), lambda i, j: (i, 0)),
                  pl.BlockSpec((None, 6, d), lambda i, j: (i // tiles_per_mod, 0, 0)),
                  pl.BlockSpec((4, d), lambda i, j: (0, 0)),
                  pl.BlockSpec((d, tf), lambda i, j: (0, j)),
                  pl.BlockSpec((tf, d), lambda i, j: (j, 0))],
        out_specs=pl.BlockSpec((tm, d), lambda i, j: (i, 0)),
        out_shape=jax.ShapeDtypeStruct((m, d), F32),
        scratch_shapes=[pltpu.VMEM((tm, d), BF16), pltpu.VMEM((tm, d), F32)],
        compiler_params=_params("parallel", "arbitrary"),
        name="mlp_residual",
    )(h, mod, g4, w1, w2)


def _hgrn2_level_operand(m, q, k, g, lf, g_ref, gpad, row, reverse):
    query_half = 0 if reverse else 1
    r_off = m if reverse else m - 1

    def boundary(rows0, n):
        r = gpad + (rows0 // (2 * m)) * (2 * m) + r_off
        return jnp.broadcast_to(g_ref[r:r + 1, :], (n, DK_A))

    if m >= SUBLANES:
        pieces = []
        for j in range(BLOCK_A // m):
            rs = slice(j * m, (j + 1) * m)
            g_r = boundary(j * m, m)
            if j % 2 == query_half:
                pieces.append(q[rs] * jnp.exp(g[rs] - g_r))
            else:
                pieces.append(k[rs] * jnp.exp(g_r - g[rs]))
        return jnp.concatenate(pieces, axis=0).astype(BF16)

    is_query = ((row // m) % 2) == query_half
    if m == 1:
        return jnp.where(is_query, q * jnp.exp(lf), k).astype(BF16)
    per_tile = SUBLANES // (2 * m)
    tiles = []
    for v in range(BLOCK_A // SUBLANES):
        g_r = boundary(v * SUBLANES, SUBLANES)
        for b in range(1, per_tile):
            in_b = ((row[0:SUBLANES] % SUBLANES) // (2 * m)) == b
            g_r = jnp.where(in_b, boundary(v * SUBLANES + b * 2 * m, SUBLANES), g_r)
        tiles.append(g_r)
    g_r = jnp.concatenate(tiles, axis=0)
    return (jnp.where(is_query, q, k) * jnp.exp(-jnp.abs(g - g_r))).astype(BF16)


def _hgrn2_scan_kernel(zq_ref, zv_ref, zf_ref, lb_ref, s0_ref, o_ref, sfin_ref,
                       st_ref, q_s, k_s, lf_s, g_s, *, reverse, tb, hpb):
    t = pl.program_id(2)
    gpad = BLOCK_A // 2
    n_blocks = tb // BLOCK_A

    @pl.when(t == 0)
    def _():
        st_ref[...] = s0_ref[...]

    zero_gpad = jnp.zeros((gpad, DK_A), F32)
    for i in range(n_blocks * hpb):
        g_s[i, 0:gpad, :] = zero_gpad
        g_s[i, gpad + BLOCK_A:gpad + BLOCK_A + gpad, :] = zero_gpad

    lb = lb_ref[...]
    zf = zf_ref[...]
    q_s[...] = _silu(zq_ref[...])
    k_s[...] = (1.0 - lb) * jax.nn.sigmoid(-zf)
    lf_s[...] = jnp.logaddexp(jnp.log(lb), jnp.log1p(-lb) + jax.nn.log_sigmoid(zf))

    ti = lax.broadcasted_iota(jnp.int32, (BLOCK_A, BLOCK_A), 0)
    si = lax.broadcasted_iota(jnp.int32, (BLOCK_A, BLOCK_A), 1)
    levels = [BLOCK_A >> i for i in range(1, BLOCK_A.bit_length())]
    same = {m: (ti // m) == (si // m) for m in levels}
    kill = (si <= ti) if reverse else (si >= ti)
    sgn = 1 if reverse else -1
    last = 0 if reverse else BLOCK_A - 1

    block_order = range(n_blocks - 1, -1, -1) if reverse else range(n_blocks)
    for blk in block_order:
        rows = slice(blk * BLOCK_A, (blk + 1) * BLOCK_A)
        for h in range(hpb):
            hs = slice(h * DK_A, (h + 1) * DK_A)
            g_ref = g_s.at[blk * hpb + h]
            q = q_s[rows, hs]
            k = k_s[rows, hs]
            lf = lf_s[rows, hs]
            v_bf = zv_ref[rows, hs].astype(BF16)

            g = lf
            shift = 1
            while shift < BLOCK_A:
                g_ref[gpad:gpad + BLOCK_A, :] = g
                off = gpad + sgn * shift
                g = g + g_ref[off:off + BLOCK_A, :]
                shift *= 2
            g_ref[gpad:gpad + BLOCK_A, :] = g

            attn = None
            for m in levels:
                x = _hgrn2_level_operand(m, q, k, g, lf, g_ref, gpad, ti, reverse)
                a = _dot_nt(x, x)
                attn = a if attn is None else jnp.where(same[2 * m], a, attn)
            attn = jnp.where(kill, 0.0, attn)

            glast = g[last:last + 1, :]
            qhat = (q * jnp.exp(g)).astype(BF16)
            khat = (k * jnp.exp(glast - g)).astype(BF16)
            st = st_ref[h]
            diag = jnp.sum(q * k, axis=-1, keepdims=True)
            o_ref[rows, hs] = (diag * zv_ref[rows, hs] + _dot(attn.astype(BF16), v_bf)
                               + _dot_nt(qhat, st.astype(BF16)))
            st_ref[h] = st * jnp.exp(glast) + _dot_tn(v_bf, khat)

    @pl.when(t == pl.num_programs(2) - 1)
    def _():
        sfin_ref[...] = st_ref[...]


def _hgrn2_scan(z, lb, s0, *, reverse, tb=256):
    bsz, t_len, _ = z.shape
    heads = s0.shape[1]
    hpb = HEADS_PER_BLOCK_A
    width = hpb * DK_A
    n_hb = heads // hpb
    tb = min(tb, t_len)
    nt = t_len // tb
    tt = (lambda t: nt - 1 - t) if reverse else (lambda t: t)
    f_base = (4 if reverse else 3) * n_hb
    kern = functools.partial(_hgrn2_scan_kernel, reverse=reverse, tb=tb, hpb=hpb)
    state_spec = pl.BlockSpec((None, hpb, DK_A, DK_A), lambda b, hb, t: (b, hb, 0, 0))
    return pl.pallas_call(
        kern,
        grid=(bsz, n_hb, nt),
        in_specs=[pl.BlockSpec((None, tb, width), lambda b, hb, t: (b, tt(t), hb)),
                  pl.BlockSpec((None, tb, width), lambda b, hb, t: (b, tt(t), 2 * n_hb + hb)),
                  pl.BlockSpec((None, tb, width), lambda b, hb, t: (b, tt(t), f_base + hb)),
                  pl.BlockSpec((1, width), lambda b, hb, t: (0, hb)),
                  state_spec],
        out_specs=[pl.BlockSpec((None, tb, width), lambda b, hb, t: (b, tt(t), hb)),
                   state_spec],
        out_shape=[jax.ShapeDtypeStruct((bsz, t_len, heads * DK_A), F32),
                   jax.ShapeDtypeStruct(s0.shape, F32)],
        scratch_shapes=[pltpu.VMEM((hpb, DK_A, DK_A), F32),
                        pltpu.VMEM((tb, width), F32),
                        pltpu.VMEM((tb, width), F32),
                        pltpu.VMEM((tb, width), F32),
                        pltpu.VMEM((tb // BLOCK_A * hpb, 2 * BLOCK_A, DK_A), F32)],
        compiler_params=_params("parallel", "parallel", "arbitrary"),
        name="hgrn2_scan_bwd" if reverse else "hgrn2_scan_fwd",
    )(z, z, z, lb, s0)


def _hgrn2_readout_kernel(of_ref, ob_ref, zg_ref, on_ref, a_ref, *, heads):
    for h in range(heads):
        hs = slice(h * DK_A, (h + 1) * DK_A)
        o = of_ref[:, hs] + ob_ref[:, hs]
        y = _rms(o) * on_ref[:, hs]
        a_ref[:, hs] = (y * _silu(zg_ref[:, hs])).astype(BF16)


def _hgrn2_readout(o_f, o_b, z2d, onorm, tm=512):
    m, d = o_f.shape
    tm = min(tm, m)
    kern = functools.partial(_hgrn2_readout_kernel, heads=d // DK_A)
    return pl.pallas_call(
        kern,
        grid=(m // tm,),
        in_specs=[pl.BlockSpec((tm, d), lambda i: (i, 0)),
                  pl.BlockSpec((tm, d), lambda i: (i, 0)),
                  pl.BlockSpec((tm, d), lambda i: (i, 1)),
                  pl.BlockSpec((1, d), lambda i: (0, 0))],
        out_specs=pl.BlockSpec((tm, d), lambda i: (i, 0)),
        out_shape=jax.ShapeDtypeStruct((m, d), BF16),
        compiler_params=_params("parallel"),
        name="hgrn2_readout",
    )(o_f, o_b, z2d, onorm)


def _rope_kernel(x_ref, cos_ref, sin_ref, o_ref, *, n_q, n_rot):
    cos = cos_ref[...]
    sin = sin_ref[...]
    lane = lax.broadcasted_iota(jnp.int32, cos.shape, 1)
    first_half = (lane % (DH_B // 2)) < ROPE_PAIRS
    for h in range(x_ref.shape[1] // DH_B):
        hs = slice(h * DH_B, (h + 1) * DH_B)
        x = x_ref[:, hs]
        if h < n_rot:
            partner = jnp.where(first_half, pltpu.roll(x, DH_B - ROPE_PAIRS, 1),
                                pltpu.roll(x, ROPE_PAIRS, 1))
            x = x * cos + partner * sin
        if h < n_q:
            x = x * SCALE_B
        o_ref[:, hs] = x.astype(BF16)


def _rope_cast(qkv, cos, sin, *, n_q, n_rot, tm=512):
    bsz, t_len, w = qkv.shape
    tm = min(tm, t_len)
    kern = functools.partial(_rope_kernel, n_q=n_q, n_rot=n_rot)
    return pl.pallas_call(
        kern,
        grid=(bsz, t_len // tm),
        in_specs=[pl.BlockSpec((None, tm, w), lambda b, i: (b, i, 0)),
                  pl.BlockSpec((tm, DH_B), lambda b, i: (i, 0)),
                  pl.BlockSpec((tm, DH_B), lambda b, i: (i, 0))],
        out_specs=pl.BlockSpec((None, tm, w), lambda b, i: (b, i, 0)),
        out_shape=jax.ShapeDtypeStruct((bsz, t_len, w), BF16),
        compiler_params=_params("parallel", "parallel"),
        name="rope_cast",
    )(qkv, cos, sin)


def _sink_column(sink_ref, kv, rows_per_head):
    return jnp.concatenate(
        [jnp.full((rows_per_head, 1), sink_ref[kv, g], F32) for g in range(GRP_B)], axis=0)


def _stack_heads(q):
    return jnp.concatenate([q[:, g * DH_B:(g + 1) * DH_B] for g in range(GRP_B)], axis=0)


def _with_ones(v):
    return jnp.concatenate([v, jnp.ones_like(v)], axis=1)


def _row_max(*scores):
    tiles = [s[:, t:t + DH_B] for s in scores for t in range(0, s.shape[1], DH_B)]
    return jnp.max(functools.reduce(jnp.maximum, tiles), axis=-1, keepdims=True)


def _swa_kernel(sink_ref, q_ref, kp_ref, kc_ref, kn_ref, vp_ref, vc_ref, vn_ref, kx_ref, vx_ref,
                o_ref, *, nq):
    kv = pl.program_id(1)
    j = pl.program_id(2)
    nj = pl.num_programs(2)
    kb = jnp.concatenate([kp_ref[...], kc_ref[...], kn_ref[...]], axis=0)
    vb = _with_ones(jnp.concatenate([vp_ref[...], vc_ref[...], vn_ref[...]], axis=0))
    kx = kx_ref[...]
    vx = _with_ones(vx_ref[...])
    sink = _sink_column(sink_ref, kv, BLK_B)
    shape = (GRP_B * BLK_B, 3 * BLK_B)
    r = lax.broadcasted_iota(jnp.int32, shape, 0) % BLK_B
    c = lax.broadcasted_iota(jnp.int32, shape, 1)
    in_window = jnp.abs(c - BLK_B - r) <= WINDOW_B
    for i in range(nq):
        rows = slice(i * BLK_B, (i + 1) * BLK_B)
        qs = _stack_heads(q_ref[rows, :])
        s_loc = _dot_nt(qs, kb[i * BLK_B:(i + 3) * BLK_B])
        valid = in_window
        if i == 0:
            valid = valid & ((c >= BLK_B) | (j > 0))
        if i == nq - 1:
            valid = valid & ((c < 2 * BLK_B) | (j < nj - 1))
        s_loc = jnp.where(valid, s_loc, -jnp.inf)
        s_ctx = _dot_nt(qs, kx)
        m = jnp.maximum(_row_max(s_loc, s_ctx), sink)
        p_loc = jnp.exp(s_loc - m)
        p_ctx = jnp.exp(s_ctx - m)
        acc = (_dot(p_loc.astype(BF16), vb[i * BLK_B:(i + 3) * BLK_B])
               + _dot(p_ctx.astype(BF16), vx))
        o = acc[:, :DH_B] / (acc[:, DH_B:DH_B + 1] + jnp.exp(sink - m))
        for g in range(GRP_B):
            o_ref[rows, g * DH_B:(g + 1) * DH_B] = o[g * BLK_B:(g + 1) * BLK_B].astype(BF16)


def _swa_attention(sink, qkv, qkv_ctx, *, n_q, n_kv):
    bsz, t_len, _ = qkv.shape
    l_len = qkv_ctx.shape[1]
    nb = t_len // BLK_B
    nq = math.gcd(nb, 4)
    gw = GRP_B * DH_B
    k0, v0 = n_q, n_q + n_kv

    def halo(col0, edge):
        return pl.BlockSpec(
            (None, BLK_B, DH_B),
            lambda b, kv, j: (b, jnp.clip(j * nq + edge, 0, nb - 1), col0 + kv))

    def cur(col0):
        return pl.BlockSpec((None, nq * BLK_B, DH_B), lambda b, kv, j: (b, j, col0 + kv))

    return pl.pallas_call(
        functools.partial(_swa_kernel, nq=nq),
        grid=(bsz, n_kv, nb // nq),
        in_specs=[pl.BlockSpec(memory_space=pltpu.SMEM),
                  pl.BlockSpec((None, nq * BLK_B, gw), lambda b, kv, j: (b, j, kv)),
                  halo(k0, -1), cur(k0), halo(k0, nq),
                  halo(v0, -1), cur(v0), halo(v0, nq),
                  pl.BlockSpec((None, l_len, DH_B), lambda b, kv, j: (b, 0, k0 + kv)),
                  pl.BlockSpec((None, l_len, DH_B), lambda b, kv, j: (b, 0, v0 + kv))],
        out_specs=pl.BlockSpec((None, nq * BLK_B, gw), lambda b, kv, j: (b, j, kv)),
        out_shape=jax.ShapeDtypeStruct((bsz, t_len, n_q * DH_B), BF16),
        compiler_params=_params("parallel", "parallel", "parallel"),
        name="swa_attention",
    )(sink, qkv, qkv, qkv, qkv, qkv, qkv, qkv, qkv_ctx, qkv_ctx)


def _ctx_attention_kernel(sink_ref, q_ref, kx_ref, vx_ref, o_ref):
    kv = pl.program_id(1)
    l_len = q_ref.shape[0]
    qs = _stack_heads(q_ref[...])
    s = _dot_nt(qs, kx_ref[...])
    sink = _sink_column(sink_ref, kv, l_len)
    m = jnp.maximum(jnp.max(s, axis=-1, keepdims=True), sink)
    p = jnp.exp(s - m)
    den = jnp.sum(p, axis=-1, keepdims=True) + jnp.exp(sink - m)
    o = _dot(p.astype(BF16), vx_ref[...]) / den
    for g in range(GRP_B):
        o_ref[:, g * DH_B:(g + 1) * DH_B] = o[g * l_len:(g + 1) * l_len].astype(BF16)


def _ctx_attention(sink, qkv_ctx, *, n_q, n_kv):
    bsz, l_len, _ = qkv_ctx.shape
    gw = GRP_B * DH_B
    return pl.pallas_call(
        _ctx_attention_kernel,
        grid=(bsz, n_kv),
        in_specs=[pl.BlockSpec(memory_space=pltpu.SMEM),
                  pl.BlockSpec((None, l_len, gw), lambda b, kv: (b, 0, kv)),
                  pl.BlockSpec((None, l_len, DH_B), lambda b, kv: (b, 0, n_q + kv)),
                  pl.BlockSpec((None, l_len, DH_B), lambda b, kv: (b, 0, n_q + n_kv + kv))],
        out_specs=pl.BlockSpec((None, l_len, gw), lambda b, kv: (b, 0, kv)),
        out_shape=jax.ShapeDtypeStruct((bsz, l_len, n_q * DH_B), BF16),
        compiler_params=_params("parallel", "parallel"),
        name="ctx_attention",
    )(sink, qkv_ctx, qkv_ctx, qkv_ctx)


def _conv_silu_kernel(prev_ref, cur_ref, next_ref, w_ref, b_ref, o_ref, ext_ref, *, tb):
    i = pl.program_id(1)
    halo = 8
    has_prev = (i > 0).astype(F32)
    has_next = (i < pl.num_programs(1) - 1).astype(F32)
    ext_ref[0:halo, :] = prev_ref[...] * has_prev
    ext_ref[halo:halo + tb, :] = cur_ref[...]
    ext_ref[halo + tb:halo + tb + halo, :] = next_ref[...] * has_next
    acc = jnp.zeros(o_ref.shape, F32) + b_ref[...]
    for tap in range(CONV_C):
        off = halo + tap - CONV_C // 2
        acc = acc + ext_ref[off:off + tb, :] * w_ref[tap:tap + 1, :]
    o_ref[...] = _silu(acc)


def _conv_silu(z, conv_w, conv_b, *, col0, tb=512, tc=512):
    bsz, t_len, _ = z.shape
    ch = conv_w.shape[1]
    tb = min(tb, t_len)
    halo = 8
    hb = tb // halo
    n_h = t_len // halo
    c0 = col0 // tc
    kern = functools.partial(_conv_silu_kernel, tb=tb)
    return pl.pallas_call(
        kern,
        grid=(bsz, t_len // tb, ch // tc),
        in_specs=[pl.BlockSpec((None, halo, tc), lambda b, i, c: (b, jnp.maximum(i * hb - 1, 0), c0 + c)),
                  pl.BlockSpec((None, tb, tc), lambda b, i, c: (b, i, c0 + c)),
                  pl.BlockSpec((None, halo, tc), lambda b, i, c: (b, jnp.minimum((i + 1) * hb, n_h - 1), c0 + c)),
                  pl.BlockSpec((CONV_C, tc), lambda b, i, c: (0, c)),
                  pl.BlockSpec((1, tc), lambda b, i, c: (0, c))],
        out_specs=pl.BlockSpec((None, tb, tc), lambda b, i, c: (b, i, c)),
        out_shape=jax.ShapeDtypeStruct((bsz, t_len, ch), F32),
        scratch_shapes=[pltpu.VMEM((tb + 2 * halo, tc), F32)],
        compiler_params=_params("parallel", "parallel", "parallel"),
        name="conv_silu",
    )(z, z, z, conv_w, conv_b)


def _ssd_scan_kernel(x_ref, b_ref, c_ref, dtr_ref, dtb_ref, an_ref, h0_ref, y_ref, hfin_ref,
                     st_ref, *, reverse, groups):
    t = pl.program_id(1)
    cl = CHUNK_C
    heads = groups * (x_ref.shape[1] // groups // P_C)
    hpg = heads // groups
    lane0 = heads if reverse else 0

    @pl.when(t == 0)
    def _():
        st_ref[...] = h0_ref[...]

    dt = jnp.logaddexp(dtr_ref[...] + dtb_ref[...], 0.0)
    la = dt * an_ref[...]
    ri = lax.broadcasted_iota(jnp.int32, (cl, cl), 0)
    ci = lax.broadcasted_iota(jnp.int32, (cl, cl), 1)
    causal = (ci >= ri) if reverse else (ci <= ri)
    cum = jnp.dot(causal.astype(F32), la, precision=lax.Precision.HIGHEST,
                  preferred_element_type=F32)
    cum_t = cum.T
    dt_t = dt.T
    last = 0 if reverse else cl - 1
    lane = lax.broadcasted_iota(jnp.int32, (cl, 2 * P_C), 1)
    lo = lane < P_C

    for g in range(groups):
        bm = b_ref[:, g * N_STATE_C:(g + 1) * N_STATE_C].astype(BF16)
        cm = c_ref[:, g * N_STATE_C:(g + 1) * N_STATE_C].astype(BF16)
        cb = _dot_nt(cm, bm)
        for pair in range(hpg // 2):
            h_a = g * hpg + 2 * pair
            xs = slice(h_a * P_C, (h_a + 2) * P_C)
            x_pair = x_ref[:, xs]
            x_bf = x_pair.astype(BF16)
            ys = []
            cols = []
            for h in (h_a, h_a + 1):
                ln = lane0 + h
                cum_col = cum[:, ln:ln + 1]
                seg = jnp.exp(jnp.where(causal, cum_col - cum_t[ln:ln + 1, :], -jnp.inf))
                mat = (cb * seg * dt_t[ln:ln + 1, :]).astype(BF16)
                ys.append(_dot(mat, x_bf))
                cols.append((cum_col, dt[:, ln:ln + 1]))
            (cum_a, dt_a), (cum_b, dt_b) = cols
            idx = g * (hpg // 2) + pair
            st = st_ref[idx]
            ecum = jnp.where(lo, jnp.exp(cum_a), jnp.exp(cum_b))
            y = jnp.where(lo, ys[0], ys[1]) + ecum * _dot(cm, st.astype(BF16))
            y_ref[:, xs] = y
            tot_a = cum_a[last:last + 1, :]
            tot_b = cum_b[last:last + 1, :]
            wgt = jnp.where(lo, jnp.exp(tot_a - cum_a) * dt_a, jnp.exp(tot_b - cum_b) * dt_b)
            etot = jnp.where(lo[0:1, :], jnp.exp(tot_a), jnp.exp(tot_b))
            st_ref[idx] = st * etot + _dot_tn(bm, (x_pair * wgt).astype(BF16))

    @pl.when(t == pl.num_programs(1) - 1)
    def _():
        hfin_ref[...] = st_ref[...]


def _ssd_scan(xbc, z, dt_bias, a_neg, h0, *, reverse, d_inner, groups, dt_col0):
    bsz, t_len, _ = xbc.shape
    n_pairs = h0.shape[1]
    nc = t_len // CHUNK_C
    gn = groups * N_STATE_C
    two_h = dt_bias.shape[1]
    tt = (lambda t: nc - 1 - t) if reverse else (lambda t: t)
    kern = functools.partial(_ssd_scan_kernel, reverse=reverse, groups=groups)
    state_spec = pl.BlockSpec((None, n_pairs, N_STATE_C, 2 * P_C), lambda b, t: (b, 0, 0, 0))
    return pl.pallas_call(
        kern,
        grid=(bsz, nc),
        in_specs=[pl.BlockSpec((None, CHUNK_C, d_inner), lambda b, t: (b, tt(t), 0)),
                  pl.BlockSpec((None, CHUNK_C, gn), lambda b, t: (b, tt(t), d_inner // gn)),
                  pl.BlockSpec((None, CHUNK_C, gn), lambda b, t: (b, tt(t), d_inner // gn + 1)),
                  pl.BlockSpec((None, CHUNK_C, two_h), lambda b, t: (b, tt(t), dt_col0 // two_h)),
                  pl.BlockSpec((1, two_h), lambda b, t: (0, 0)),
                  pl.BlockSpec((1, two_h), lambda b, t: (0, 0)),
                  state_spec],
        out_specs=[pl.BlockSpec((None, CHUNK_C, d_inner), lambda b, t: (b, tt(t), 0)),
                   state_spec],
        out_shape=[jax.ShapeDtypeStruct((bsz, t_len, d_inner), F32),
                   jax.ShapeDtypeStruct(h0.shape, F32)],
        scratch_shapes=[pltpu.VMEM(h0.shape[1:], F32)],
        compiler_params=_params("parallel", "arbitrary"),
        name="ssd_scan_bwd" if reverse else "ssd_scan_fwd",
    )(xbc, xbc, xbc, z, dt_bias, a_neg, h0)


def _ssd_readout_kernel(yf_ref, yb_ref, x_ref, z_ref, d_ref, on_ref, a_ref, *, groups):
    gw = a_ref.shape[1] // groups
    for g in range(groups):
        gs = slice(g * gw, (g + 1) * gw)
        y = (yf_ref[:, gs] + yb_ref[:, gs] + d_ref[:, gs] * x_ref[:, gs]) * _silu(z_ref[:, gs])
        a_ref[:, gs] = (_rms(y) * on_ref[:, gs]).astype(BF16)


def _ssd_readout(y_f, y_b, xbc2d, z2d, d_exp, onorm, *, groups, tm=256):
    m, d_inner = y_f.shape
    tm = min(tm, m)
    kern = functools.partial(_ssd_readout_kernel, groups=groups)
    row = lambda i: (i, 0)
    return pl.pallas_call(
        kern,
        grid=(m // tm,),
        in_specs=[pl.BlockSpec((tm, d_inner), row),
                  pl.BlockSpec((tm, d_inner), row),
                  pl.BlockSpec((tm, d_inner), row),
                  pl.BlockSpec((tm, d_inner), row),
                  pl.BlockSpec((1, d_inner), lambda i: (0, 0)),
                  pl.BlockSpec((1, d_inner), lambda i: (0, 0))],
        out_specs=pl.BlockSpec((tm, d_inner), row),
        out_shape=jax.ShapeDtypeStruct((m, d_inner), BF16),
        compiler_params=_params("parallel"),
        name="ssd_readout",
    )(y_f, y_b, xbc2d, z2d, d_exp, onorm)


def _in_proj(h, mod, g4, w, *, rows_per_mod, tn):
    tm = 1024 if tn <= 512 else 512
    return _modlinear(h, mod, g4, w, g_row=0, shift_row=0, scale_row=1,
                      rows_per_mod=rows_per_mod, tn=tn, tm=tm)


def _hgrn2_mixer(zc, zl, lb, onorm, emit_ctx):
    bsz, t_len, five_d = zl.shape
    d_a = five_d // 5
    heads = d_a // DK_A
    l_len = zc.shape[1]
    s0 = jnp.zeros((bsz, heads, DK_A, DK_A), F32)
    o_lat, o_ctx = [], []
    for d in range(2):
        lb_d = lb[d].reshape(1, d_a)
        oc, s_ctx = _hgrn2_scan(zc, lb_d, s0, reverse=bool(d))
        ol, _ = _hgrn2_scan(zl, lb_d, s_ctx, reverse=bool(d))
        o_ctx.append(oc)
        o_lat.append(ol)
    on = onorm.reshape(1, d_a)
    al = _hgrn2_readout(o_lat[0].reshape(bsz * t_len, d_a), o_lat[1].reshape(bsz * t_len, d_a),
                        zl.reshape(bsz * t_len, five_d), on)
    ac = None
    if emit_ctx:
        ac = _hgrn2_readout(o_ctx[0].reshape(bsz * l_len, d_a), o_ctx[1].reshape(bsz * l_len, d_a),
                            zc.reshape(bsz * l_len, five_d), on)
    return ac, al


def _rope_tables(t_len):
    rows = t_len // GRID_W
    inv_freq = ROPE_BASE ** (-jnp.arange(ROPE_PAIRS, dtype=F32) / ROPE_PAIRS)
    row = jnp.repeat(jnp.arange(rows, dtype=F32), GRID_W)
    col = jnp.tile(jnp.arange(GRID_W, dtype=F32), rows)
    ang_row = row[:, None] * inv_freq
    ang_col = col[:, None] * inv_freq
    cos = jnp.concatenate([jnp.cos(ang_row)] * 2 + [jnp.cos(ang_col)] * 2, axis=-1)
    sin = jnp.concatenate([-jnp.sin(ang_row), jnp.sin(ang_row),
                           -jnp.sin(ang_col), jnp.sin(ang_col)], axis=-1)
    return cos, sin


def _swa_mixer(zc, zl, sink, emit_ctx):
    bsz, t_len, w = zl.shape
    l_len = zc.shape[1]
    n_kv = sink.shape[0] // GRP_B
    n_q = sink.shape[0]
    cos, sin = _rope_tables(t_len)
    ql = _rope_cast(zl, cos, sin, n_q=n_q, n_rot=n_q + n_kv)
    ones = jnp.ones((l_len, DH_B), F32)
    qc = _rope_cast(zc, ones, ones, n_q=n_q, n_rot=0)
    sink2 = sink.astype(F32).reshape(n_kv, GRP_B)
    al = _swa_attention(sink2, ql, qc, n_q=n_q, n_kv=n_kv).reshape(bsz * t_len, n_q * DH_B)
    ac = None
    if emit_ctx:
        ac = _ctx_attention(sink2, qc, n_q=n_q, n_kv=n_kv).reshape(bsz * l_len, n_q * DH_B)
    return ac, al


def _ssd_mixer(zc, zl, conv_w, conv_b, dt_bias, a_log, d_skip, onorm, emit_ctx):
    bsz = zl.shape[0]
    d_inner = onorm.shape[0]
    heads = d_skip.shape[0]
    groups = N_GROUPS_C
    gn = groups * N_STATE_C
    a_neg = (-jnp.exp(a_log.astype(F32))).reshape(1, 2 * heads)
    dtb = dt_bias.astype(F32).reshape(1, 2 * heads)
    d_exp = jnp.repeat(d_skip.astype(F32), P_C).reshape(1, d_inner)
    on = onorm.reshape(1, d_inner)
    cb = conv_b.reshape(1, -1)
    xbc_c = _conv_silu(zc, conv_w, cb, col0=d_inner)
    xbc_l = _conv_silu(zl, conv_w, cb, col0=d_inner)
    h0 = jnp.zeros((bsz, heads // 2, N_STATE_C, 2 * P_C), F32)
    kw = dict(d_inner=d_inner, groups=groups, dt_col0=2 * d_inner + 2 * gn)
    y_lat, y_ctx = [], []
    for d in range(2):
        yc, h_ctx = _ssd_scan(xbc_c, zc, dtb, a_neg, h0, reverse=bool(d), **kw)
        yl, _ = _ssd_scan(xbc_l, zl, dtb, a_neg, h_ctx, reverse=bool(d), **kw)
        y_ctx.append(yc)
        y_lat.append(yl)

    def readout(ys, xbc, z):
        m = z.shape[0] * z.shape[1]
        return _ssd_readout(ys[0].reshape(m, d_inner), ys[1].reshape(m, d_inner),
                            xbc.reshape(m, -1), z.reshape(m, -1), d_exp, on, groups=groups)

    ac = readout(y_ctx, xbc_c, zc) if emit_ctx else None
    return ac, readout(y_lat, xbc_l, zl)


def kernel(x, c, ctx, c_ctx, ada_w, ada_b, norm_g, mlp_w1, mlp_w2, a_w_in, a_lb_logits, a_onorm,
           a_w_out, b_w_qkv, b_sink, b_w_out, c_w_in, c_conv_w, c_conv_b, c_dt_bias, c_a_log, c_d,
           c_onorm, c_w_out):
    bsz, t_len, d = x.shape
    l_len = ctx.shape[1]
    depth = ada_w.shape[0]

    p_lb = jax.nn.softmax(a_lb_logits.astype(F32), axis=0)
    lower_bounds = jnp.cumsum(p_lb, axis=0) - p_lb

    c_rows = jnp.concatenate([c, c_ctx[None, :], jnp.zeros((8 - bsz - 1, d), F32)], axis=0)
    mods = _ada(c_rows, ada_w, ada_b).reshape(depth, 8, 6, d)

    hl = x.reshape(bsz * t_len, d)
    hc = ctx.reshape(bsz * l_len, d)
    rows_l, rows_c = t_len, bsz * l_len
    for i in range(depth):
        emit_ctx = i < depth - 1
        m_l = mods[i, :bsz]
        m_c = mods[i, bsz:bsz + 1]
        g4 = norm_g[i]
        kind, j = i % N_MIXERS, i // N_MIXERS
        if kind == 0:
            w_in, w_out, tn = a_w_in[j].astype(BF16), a_w_out[j].astype(BF16), 512
        elif kind == 1:
            w_in, w_out, tn = b_w_qkv[j].astype(BF16), b_w_out[j].astype(BF16), 512
        else:
            w_in, w_out, tn = c_w_in[j].astype(BF16), c_w_out[j].astype(BF16), 1152
        zl = _in_proj(hl, m_l, g4, w_in, rows_per_mod=rows_l, tn=tn).reshape(bsz, t_len, -1)
        zc = _in_proj(hc, m_c, g4, w_in, rows_per_mod=rows_c, tn=tn).reshape(bsz, l_len, -1)
        if kind == 0:
            ac, al = _hgrn2_mixer(zc, zl, lower_bounds[i], a_onorm[j], emit_ctx)
        elif kind == 1:
            ac, al = _swa_mixer(zc, zl, b_sink[j], emit_ctx)
        else:
            ac, al = _ssd_mixer(zc, zl, c_conv_w[j], c_conv_b[j], c_dt_bias[j], c_a_log[j], c_d[j],
                                c_onorm[j], emit_ctx)
        w1 = mlp_w1[i].astype(BF16)
        w2 = mlp_w2[i].astype(BF16)
        hl = _linear_resnorm(al, w_out, hl, m_l, g4, g_row=1, gate_row=2, rows_per_mod=rows_l)
        hl = _mlp(hl, m_l, g4, w1, w2, rows_per_mod=rows_l)
        if emit_ctx:
            hc = _linear_resnorm(ac, w_out, hc, m_c, g4, g_row=1, gate_row=2, rows_per_mod=rows_c)
            hc = _mlp(hc, m_c, g4, w1, w2, rows_per_mod=rows_c)
    return hl.reshape(bsz, t_len, d)
```

```python
import functools
import math

import jax
import jax.numpy as jnp
from jax import lax
from jax.experimental import pallas as pl
from jax.experimental.pallas import tpu as pltpu

F32 = jnp.float32
BF16 = jnp.bfloat16

EPS = 1e-6
GRID_W = 64
ROPE_BASE = 10000.0
N_MIXERS = 3

DK_A = 128
BLOCK_A = 128
HEADS_PER_BLOCK_A = 4
SUBLANES = 8
LOG2E = math.log2(math.e)

DH_B = 128
GRP_B = 4
WINDOW_B = 128
BLK_B = 128
SCALE_B = DH_B ** -0.5
ROPE_PAIRS = DH_B // 4

P_C = 64
N_GROUPS_C = 8
N_STATE_C = 128
CONV_C = 5
CHUNK_C = 128
CUM_PARTS_C = 3
DT_PARTS_C = 2

VMEM_LIMIT = 48 * 1024 * 1024


def _params(*sem):
    return pltpu.CompilerParams(dimension_semantics=sem, vmem_limit_bytes=VMEM_LIMIT)


def _rms(x):
    return x * lax.rsqrt(jnp.mean(x * x, axis=-1, keepdims=True) + EPS)


def _silu(x):
    return x * jax.nn.sigmoid(x)


def _dot(a, b):
    return jnp.dot(a, b, preferred_element_type=F32)


def _dot_nt(a, b):
    return lax.dot_general(a, b, (((1,), (1,)), ((), ())), preferred_element_type=F32)


def _dot_tn(a, b):
    return lax.dot_general(a, b, (((0,), (0,)), ((), ())), preferred_element_type=F32)


def _ada_kernel(c_ref, w_ref, b_ref, o_ref):
    s = _silu(c_ref[...])
    o_ref[...] = jnp.dot(s, w_ref[...], precision=lax.Precision.HIGHEST,
                         preferred_element_type=F32) + b_ref[...]


def _ada(c_rows, ada_w, ada_b, tn=768):
    depth, d, n = ada_w.shape
    rows = c_rows.shape[0]
    return pl.pallas_call(
        _ada_kernel,
        grid=(depth, n // tn),
        in_specs=[pl.BlockSpec((rows, d), lambda l, j: (0, 0)),
                  pl.BlockSpec((None, d, tn), lambda l, j: (l, 0, j)),
                  pl.BlockSpec((None, 1, tn), lambda l, j: (l, 0, j))],
        out_specs=pl.BlockSpec((None, rows, tn), lambda l, j: (l, 0, j)),
        out_shape=jax.ShapeDtypeStruct((depth, rows, n), F32),
        compiler_params=_params("parallel", "parallel"),
        name="ada_modulation",
    )(c_rows, ada_w, ada_b.reshape(depth, 1, n))


def _modlinear_kernel(h_ref, mod_ref, g_ref, w_ref, o_ref, u_ref, *, g_row, shift_row, scale_row):
    @pl.when(pl.program_id(1) == 0)
    def _():
        y = _rms(h_ref[...]) * g_ref[g_row:g_row + 1, :]
        u = y * (1.0 + mod_ref[scale_row:scale_row + 1, :]) + mod_ref[shift_row:shift_row + 1, :]
        u_ref[...] = u.astype(BF16)

    o_ref[...] = _dot(u_ref[...], w_ref[...]).astype(o_ref.dtype)


def _modlinear(h, mod, g4, w, *, g_row, shift_row, scale_row, rows_per_mod, tn, tm=1024):
    m, d = h.shape
    n = w.shape[1]
    tm = min(tm, m, rows_per_mod)
    tiles_per_mod = rows_per_mod // tm
    kern = functools.partial(_modlinear_kernel, g_row=g_row, shift_row=shift_row, scale_row=scale_row)
    return pl.pallas_call(
        kern,
        grid=(m // tm, n // tn),
        in_specs=[pl.BlockSpec((tm, d), lambda i, j: (i, 0)),
                  pl.BlockSpec((None, 6, d), lambda i, j: (i // tiles_per_mod, 0, 0)),
                  pl.BlockSpec((4, d), lambda i, j: (0, 0)),
                  pl.BlockSpec((d, tn), lambda i, j: (0, j))],
        out_specs=pl.BlockSpec((tm, tn), lambda i, j: (i, j)),
        out_shape=jax.ShapeDtypeStruct((m, n), F32),
        scratch_shapes=[pltpu.VMEM((tm, d), BF16)],
        compiler_params=_params("parallel", "arbitrary"),
        name="modulate_in_proj",
    )(h, mod, g4, w)


def _linear_resnorm_kernel(a_ref, w_ref, h_ref, mod_ref, g_ref, o_ref, acc_ref, *, g_row, gate_row):
    k = pl.program_id(1)

    @pl.when(k == 0)
    def _():
        acc_ref[...] = jnp.zeros_like(acc_ref)

    acc_ref[...] += _dot(a_ref[...], w_ref[...])

    @pl.when(k == pl.num_programs(1) - 1)
    def _():
        y = _rms(acc_ref[...]) * g_ref[g_row:g_row + 1, :]
        o_ref[...] = h_ref[...] + mod_ref[gate_row:gate_row + 1, :] * y


def _linear_resnorm(a, w, h, mod, g4, *, g_row, gate_row, rows_per_mod, tm=512, tk=2048):
    m, kdim = a.shape
    d = w.shape[1]
    tm = min(tm, m, rows_per_mod)
    tk = min(tk, kdim)
    tiles_per_mod = rows_per_mod // tm
    kern = functools.partial(_linear_resnorm_kernel, g_row=g_row, gate_row=gate_row)
    return pl.pallas_call(
        kern,
        grid=(m // tm, kdim // tk),
        in_specs=[pl.BlockSpec((tm, tk), lambda i, k: (i, k)),
                  pl.BlockSpec((tk, d), lambda i, k: (k, 0)),
                  pl.BlockSpec((tm, d), lambda i, k: (i, 0)),
                  pl.BlockSpec((None, 6, d), lambda i, k: (i // tiles_per_mod, 0, 0)),
                  pl.BlockSpec((4, d), lambda i, k: (0, 0))],
        out_specs=pl.BlockSpec((tm, d), lambda i, k: (i, 0)),
        out_shape=jax.ShapeDtypeStruct((m, d), F32),
        scratch_shapes=[pltpu.VMEM((tm, d), F32)],
        compiler_params=_params("parallel", "arbitrary"),
        name="out_proj_residual",
    )(a, w, h, mod, g4)


def _mlp_kernel(h_ref, mod_ref, g_ref, w1_ref, w2_ref, o_ref, u_ref, acc_ref):
    j = pl.program_id(1)

    @pl.when(j == 0)
    def _():
        y = _rms(h_ref[...]) * g_ref[2:3, :]
        u_ref[...] = (y * (1.0 + mod_ref[4:5, :]) + mod_ref[3:4, :]).astype(BF16)
        acc_ref[...] = jnp.zeros_like(acc_ref)

    a = jnp.maximum(_dot(u_ref[...], w1_ref[...]), 0.0)
    acc_ref[...] += _dot((a * a).astype(BF16), w2_ref[...])

    @pl.when(j == pl.num_programs(1) - 1)
    def _():
        y = _rms(acc_ref[...]) * g_ref[3:4, :]
        o_ref[...] = h_ref[...] + mod_ref[5:6, :] * y


def _mlp(h, mod, g4, w1, w2, *, rows_per_mod, tm=512, tf=1024):
    m, d = h.shape
    ff = w1.shape[1]
    tm = min(tm, m, rows_per_mod)
    tiles_per_mod = rows_per_mod // tm
    return pl.pallas_call(
        _mlp_kernel,
        grid=(m // tm, ff // tf),
        in_specs=[pl.BlockSpec((tm, d), lambda i, j: (i, 0)),
                  pl.BlockSpec((None, 6, d), lambda i, j: (i // tiles_per_mod, 0, 0)),
                  pl.BlockSpec((4, d), lambda i, j: (0, 0)),
                  pl.BlockSpec((d, tf), lambda i, j: (0, j)),
                  pl.BlockSpec((tf, d), lambda i, j: (j, 0))],
        out_specs=pl.BlockSpec((tm, d), lambda i, j: (i, 0)),
        out_shape=jax.ShapeDtypeStruct((m, d), F32),
        scratch_shapes=[pltpu.VMEM((tm, d), BF16), pltpu.VMEM((tm, d), F32)],
        compiler_params=_params("parallel", "arbitrary"),
        name="mlp_residual",
    )(h, mod, g4, w1, w2)


def _hgrn2_level_operand(m, q, k, g, lf, g_ref, gpad, row, reverse):
    query_half = 0 if reverse else 1
    r_off = m if reverse else m - 1

    def boundary(rows0, n):
        r = gpad + (rows0 // (2 * m)) * (2 * m) + r_off
        return jnp.broadcast_to(g_ref[r:r + 1, :], (n, DK_A))

    if m >= SUBLANES:
        pieces = []
        for j in range(BLOCK_A // m):
            rs = slice(j * m, (j + 1) * m)
            g_r = boundary(j * m, m)
            if j % 2 == query_half:
                pieces.append(q[rs] * jnp.exp2(g[rs] - g_r))
            else:
                pieces.append(k[rs] * jnp.exp2(g_r - g[rs]))
        return jnp.concatenate(pieces, axis=0).astype(BF16)

    is_query = ((row // m) % 2) == query_half
    if m == 1:
        return jnp.where(is_query, q * jnp.exp2(lf), k).astype(BF16)
    per_tile = SUBLANES // (2 * m)
    tiles = []
    for v in range(BLOCK_A // SUBLANES):
        g_r = boundary(v * SUBLANES, SUBLANES)
        for b in range(1, per_tile):
            in_b = ((row[0:SUBLANES] % SUBLANES) // (2 * m)) == b
            g_r = jnp.where(in_b, boundary(v * SUBLANES + b * 2 * m, SUBLANES), g_r)
        tiles.append(g_r)
    g_r = jnp.concatenate(tiles, axis=0)
    return (jnp.where(is_query, q, k) * jnp.exp2(-jnp.abs(g - g_r))).astype(BF16)


def _hgrn2_scan_kernel(zq_ref, zv_ref, zf_ref, lb_ref, s0_ref, o_ref, sfin_ref,
                       st_ref, q_s, k_s, lf_s, g_s, *, reverse, tb, hpb):
    t = pl.program_id(2)
    gpad = BLOCK_A // 2
    n_blocks = tb // BLOCK_A

    @pl.when(t == 0)
    def _():
        st_ref[...] = s0_ref[...]

    zero_gpad = jnp.zeros((gpad, DK_A), F32)
    for i in range(n_blocks * hpb):
        g_s[i, 0:gpad, :] = zero_gpad
        g_s[i, gpad + BLOCK_A:gpad + BLOCK_A + gpad, :] = zero_gpad

    lb = lb_ref[...]
    zf = zf_ref[...]
    q_s[...] = _silu(zq_ref[...])
    k_s[...] = (1.0 - lb) * jax.nn.sigmoid(-zf)
    lf_s[...] = LOG2E * jnp.logaddexp(jnp.log(lb), jnp.log1p(-lb) + jax.nn.log_sigmoid(zf))

    ti = lax.broadcasted_iota(jnp.int32, (BLOCK_A, BLOCK_A), 0)
    si = lax.broadcasted_iota(jnp.int32, (BLOCK_A, BLOCK_A), 1)
    levels = [BLOCK_A >> i for i in range(1, BLOCK_A.bit_length())]
    same = {m: (ti // m) == (si // m) for m in levels}
    kill = (si <= ti) if reverse else (si >= ti)
    sgn = 1 if reverse else -1
    last = 0 if reverse else BLOCK_A - 1

    block_order = range(n_blocks - 1, -1, -1) if reverse else range(n_blocks)
    units = [(blk, h) for blk in block_order for h in range(hpb)]
    view = {u: (slice(u[0] * BLOCK_A, (u[0] + 1) * BLOCK_A), slice(u[1] * DK_A, (u[1] + 1) * DK_A))
            for u in units}

    g = {u: lf_s[view[u]] for u in units}
    shift = 1
    while shift < BLOCK_A:
        for u in units:
            g_ref = g_s.at[u[0] * hpb + u[1]]
            g_ref[gpad:gpad + BLOCK_A, :] = g[u]
            off = gpad + sgn * shift
            g[u] = g[u] + g_ref[off:off + BLOCK_A, :]
        shift *= 2
    for u in units:
        g_s[u[0] * hpb + u[1], gpad:gpad + BLOCK_A, :] = g[u]

    attn = {}
    for m in levels:
        for u in units:
            g_ref = g_s.at[u[0] * hpb + u[1]]
            x = _hgrn2_level_operand(m, q_s[view[u]], k_s[view[u]], g[u], lf_s[view[u]], g_ref, gpad,
                                     ti, reverse)
            a = _dot_nt(x, x)
            attn[u] = jnp.where(same[2 * m], a, attn[u]) if u in attn else a

    for u in units:
        h = u[1]
        q = q_s[view[u]]
        k = k_s[view[u]]
        v = zv_ref[view[u]]
        v_bf = v.astype(BF16)
        glast = g[u][last:last + 1, :]
        qhat = (q * jnp.exp2(g[u])).astype(BF16)
        khat = (k * jnp.exp2(glast - g[u])).astype(BF16)
        st = st_ref[h]
        diag = jnp.sum(q * k, axis=-1, keepdims=True)
        o_ref[view[u]] = (diag * v + _dot(jnp.where(kill, 0.0, attn[u]).astype(BF16), v_bf)
                          + _dot_nt(qhat, st.astype(BF16)))
        st_ref[h] = st * jnp.exp2(glast) + _dot_tn(v_bf, khat)

    @pl.when(t == pl.num_programs(2) - 1)
    def _():
        sfin_ref[...] = st_ref[...]


def _hgrn2_scan(z, lb, s0, *, reverse, tb=256):
    bsz, t_len, _ = z.shape
    heads = s0.shape[1]
    hpb = HEADS_PER_BLOCK_A
    width = hpb * DK_A
    n_hb = heads // hpb
    tb = min(tb, t_len)
    nt = t_len // tb
    tt = (lambda t: nt - 1 - t) if reverse else (lambda t: t)
    f_base = (4 if reverse else 3) * n_hb
    kern = functools.partial(_hgrn2_scan_kernel, reverse=reverse, tb=tb, hpb=hpb)
    state_spec = pl.BlockSpec((None, hpb, DK_A, DK_A), lambda b, hb, t: (b, hb, 0, 0))
    return pl.pallas_call(
        kern,
        grid=(bsz, n_hb, nt),
        in_specs=[pl.BlockSpec((None, tb, width), lambda b, hb, t: (b, tt(t), hb)),
                  pl.BlockSpec((None, tb, width), lambda b, hb, t: (b, tt(t), 2 * n_hb + hb)),
                  pl.BlockSpec((None, tb, width), lambda b, hb, t: (b, tt(t), f_base + hb)),
                  pl.BlockSpec((1, width), lambda b, hb, t: (0, hb)),
                  state_spec],
        out_specs=[pl.BlockSpec((None, tb, width), lambda b, hb, t: (b, tt(t), hb)),
                   state_spec],
        out_shape=[jax.ShapeDtypeStruct((bsz, t_len, heads * DK_A), F32),
                   jax.ShapeDtypeStruct(s0.shape, F32)],
        scratch_shapes=[pltpu.VMEM((hpb, DK_A, DK_A), F32),
                        pltpu.VMEM((tb, width), F32),
                        pltpu.VMEM((tb, width), F32),
                        pltpu.VMEM((tb, width), F32),
                        pltpu.VMEM((tb // BLOCK_A * hpb, 2 * BLOCK_A, DK_A), F32)],
        compiler_params=_params("parallel", "parallel", "arbitrary"),
        name="hgrn2_scan_bwd" if reverse else "hgrn2_scan_fwd",
    )(z, z, z, lb, s0)


def _hgrn2_readout_kernel(of_ref, ob_ref, zg_ref, on_ref, a_ref, *, heads):
    for h in range(heads):
        hs = slice(h * DK_A, (h + 1) * DK_A)
        o = of_ref[:, hs] + ob_ref[:, hs]
        y = _rms(o) * on_ref[:, hs]
        a_ref[:, hs] = (y * _silu(zg_ref[:, hs])).astype(BF16)


def _hgrn2_readout(o_f, o_b, z2d, onorm, tm=512):
    m, d = o_f.shape
    tm = min(tm, m)
    kern = functools.partial(_hgrn2_readout_kernel, heads=d // DK_A)
    return pl.pallas_call(
        kern,
        grid=(m // tm,),
        in_specs=[pl.BlockSpec((tm, d), lambda i: (i, 0)),
                  pl.BlockSpec((tm, d), lambda i: (i, 0)),
                  pl.BlockSpec((tm, d), lambda i: (i, 1)),
                  pl.BlockSpec((1, d), lambda i: (0, 0))],
        out_specs=pl.BlockSpec((tm, d), lambda i: (i, 0)),
        out_shape=jax.ShapeDtypeStruct((m, d), BF16),
        compiler_params=_params("parallel"),
        name="hgrn2_readout",
    )(o_f, o_b, z2d, onorm)


def _rope_kernel(x_ref, cos_ref, sin_ref, o_ref, *, n_q, n_rot):
    cos = cos_ref[...]
    sin = sin_ref[...]
    lane = lax.broadcasted_iota(jnp.int32, cos.shape, 1)
    first_half = (lane % (DH_B // 2)) < ROPE_PAIRS
    for h in range(x_ref.shape[1] // DH_B):
        hs = slice(h * DH_B, (h + 1) * DH_B)
        x = x_ref[:, hs]
        if h < n_rot:
            partner = jnp.where(first_half, pltpu.roll(x, DH_B - ROPE_PAIRS, 1),
                                pltpu.roll(x, ROPE_PAIRS, 1))
            x = x * cos + partner * sin
        if h < n_q:
            x = x * SCALE_B
        o_ref[:, hs] = x.astype(BF16)


def _rope_cast(qkv, cos, sin, *, n_q, n_rot, tm=512):
    bsz, t_len, w = qkv.shape
    tm = min(tm, t_len)
    kern = functools.partial(_rope_kernel, n_q=n_q, n_rot=n_rot)
    return pl.pallas_call(
        kern,
        grid=(bsz, t_len // tm),
        in_specs=[pl.BlockSpec((None, tm, w), lambda b, i: (b, i, 0)),
                  pl.BlockSpec((tm, DH_B), lambda b, i: (i, 0)),
                  pl.BlockSpec((tm, DH_B), lambda b, i: (i, 0))],
        out_specs=pl.BlockSpec((None, tm, w), lambda b, i: (b, i, 0)),
        out_shape=jax.ShapeDtypeStruct((bsz, t_len, w), BF16),
        compiler_params=_params("parallel", "parallel"),
        name="rope_cast",
    )(qkv, cos, sin)


def _sink_column(sink_ref, kv, rows_per_head):
    return jnp.concatenate(
        [jnp.full((rows_per_head, 1), sink_ref[kv, g], F32) for g in range(GRP_B)], axis=0)


def _stack_heads(q):
    return jnp.concatenate([q[:, g * DH_B:(g + 1) * DH_B] for g in range(GRP_B)], axis=0)


def _with_ones(v):
    return jnp.concatenate([v, jnp.ones_like(v)], axis=1)


def _row_max(*scores):
    tiles = [s[:, t:t + DH_B] for s in scores for t in range(0, s.shape[1], DH_B)]
    return jnp.max(functools.reduce(jnp.maximum, tiles), axis=-1, keepdims=True)


def _swa_kernel(sink_ref, q_ref, kp_ref, kc_ref, kn_ref, vp_ref, vc_ref, vn_ref, kx_ref, vx_ref,
                o_ref, *, nq):
    kv = pl.program_id(1)
    j = pl.program_id(2)
    nj = pl.num_programs(2)
    kb = jnp.concatenate([kp_ref[...], kc_ref[...], kn_ref[...]], axis=0)
    vb = _with_ones(jnp.concatenate([vp_ref[...], vc_ref[...], vn_ref[...]], axis=0))
    kx = kx_ref[...]
    vx = _with_ones(vx_ref[...])
    sink = _sink_column(sink_ref, kv, BLK_B)
    shape = (GRP_B * BLK_B, 3 * BLK_B)
    r = lax.broadcasted_iota(jnp.int32, shape, 0) % BLK_B
    c = lax.broadcasted_iota(jnp.int32, shape, 1)
    in_window = jnp.abs(c - BLK_B - r) <= WINDOW_B
    for i in range(nq):
        rows = slice(i * BLK_B, (i + 1) * BLK_B)
        qs = _stack_heads(q_ref[rows, :])
        s_loc = _dot_nt(qs, kb[i * BLK_B:(i + 3) * BLK_B])
        valid = in_window
        if i == 0:
            valid = valid & ((c >= BLK_B) | (j > 0))
        if i == nq - 1:
            valid = valid & ((c < 2 * BLK_B) | (j < nj - 1))
        s_loc = jnp.where(valid, s_loc, -jnp.inf)
        s_ctx = _dot_nt(qs, kx)
        m = jnp.maximum(_row_max(s_loc, s_ctx), sink)
        p_loc = jnp.exp(s_loc - m)
        p_ctx = jnp.exp(s_ctx - m)
        acc = (_dot(p_loc.astype(BF16), vb[i * BLK_B:(i + 3) * BLK_B])
               + _dot(p_ctx.astype(BF16), vx))
        o = acc[:, :DH_B] / (acc[:, DH_B:DH_B + 1] + jnp.exp(sink - m))
        for g in range(GRP_B):
            o_ref[rows, g * DH_B:(g + 1) * DH_B] = o[g * BLK_B:(g + 1) * BLK_B].astype(BF16)


def _swa_attention(sink, qkv, qkv_ctx, *, n_q, n_kv):
    bsz, t_len, _ = qkv.shape
    l_len = qkv_ctx.shape[1]
    nb = t_len // BLK_B
    nq = math.gcd(nb, 4)
    gw = GRP_B * DH_B
    k0, v0 = n_q, n_q + n_kv

    def halo(col0, edge):
        return pl.BlockSpec(
            (None, BLK_B, DH_B),
            lambda b, kv, j: (b, jnp.clip(j * nq + edge, 0, nb - 1), col0 + kv))

    def cur(col0):
        return pl.BlockSpec((None, nq * BLK_B, DH_B), lambda b, kv, j: (b, j, col0 + kv))

    return pl.pallas_call(
        functools.partial(_swa_kernel, nq=nq),
        grid=(bsz, n_kv, nb // nq),
        in_specs=[pl.BlockSpec(memory_space=pltpu.SMEM),
                  pl.BlockSpec((None, nq * BLK_B, gw), lambda b, kv, j: (b, j, kv)),
                  halo(k0, -1), cur(k0), halo(k0, nq),
                  halo(v0, -1), cur(v0), halo(v0, nq),
                  pl.BlockSpec((None, l_len, DH_B), lambda b, kv, j: (b, 0, k0 + kv)),
                  pl.BlockSpec((None, l_len, DH_B), lambda b, kv, j: (b, 0, v0 + kv))],
        out_specs=pl.BlockSpec((None, nq * BLK_B, gw), lambda b, kv, j: (b, j, kv)),
        out_shape=jax.ShapeDtypeStruct((bsz, t_len, n_q * DH_B), BF16),
        compiler_params=_params("parallel", "parallel", "parallel"),
        name="swa_attention",
    )(sink, qkv, qkv, qkv, qkv, qkv, qkv, qkv, qkv_ctx, qkv_ctx)


def _ctx_attention_kernel(sink_ref, q_ref, kx_ref, vx_ref, o_ref):
    kv = pl.program_id(1)
    l_len = q_ref.shape[0]
    qs = _stack_heads(q_ref[...])
    s = _dot_nt(qs, kx_ref[...])
    sink = _sink_column(sink_ref, kv, l_len)
    m = jnp.maximum(jnp.max(s, axis=-1, keepdims=True), sink)
    p = jnp.exp(s - m)
    den = jnp.sum(p, axis=-1, keepdims=True) + jnp.exp(sink - m)
    o = _dot(p.astype(BF16), vx_ref[...]) / den
    for g in range(GRP_B):
        o_ref[:, g * DH_B:(g + 1) * DH_B] = o[g * l_len:(g + 1) * l_len].astype(BF16)


def _ctx_attention(sink, qkv_ctx, *, n_q, n_kv):
    bsz, l_len, _ = qkv_ctx.shape
    gw = GRP_B * DH_B
    return pl.pallas_call(
        _ctx_attention_kernel,
        grid=(bsz, n_kv),
        in_specs=[pl.BlockSpec(memory_space=pltpu.SMEM),
                  pl.BlockSpec((None, l_len, gw), lambda b, kv: (b, 0, kv)),
                  pl.BlockSpec((None, l_len, DH_B), lambda b, kv: (b, 0, n_q + kv)),
                  pl.BlockSpec((None, l_len, DH_B), lambda b, kv: (b, 0, n_q + n_kv + kv))],
        out_specs=pl.BlockSpec((None, l_len, gw), lambda b, kv: (b, 0, kv)),
        out_shape=jax.ShapeDtypeStruct((bsz, l_len, n_q * DH_B), BF16),
        compiler_params=_params("parallel", "parallel"),
        name="ctx_attention",
    )(sink, qkv_ctx, qkv_ctx, qkv_ctx)


def _conv_silu_kernel(prev_ref, cur_ref, next_ref, w_ref, b_ref, o_ref, ext_ref, *, tb):
    i = pl.program_id(1)
    halo = 8
    has_prev = (i > 0).astype(F32)
    has_next = (i < pl.num_programs(1) - 1).astype(F32)
    ext_ref[0:halo, :] = prev_ref[...] * has_prev
    ext_ref[halo:halo + tb, :] = cur_ref[...]
    ext_ref[halo + tb:halo + tb + halo, :] = next_ref[...] * has_next
    acc = jnp.zeros(o_ref.shape, F32) + b_ref[...]
    for tap in range(CONV_C):
        off = halo + tap - CONV_C // 2
        acc = acc + ext_ref[off:off + tb, :] * w_ref[tap:tap + 1, :]
    o_ref[...] = _silu(acc)


def _conv_silu(z, conv_w, conv_b, *, col0, tb=512, tc=512):
    bsz, t_len, _ = z.shape
    ch = conv_w.shape[1]
    tb = min(tb, t_len)
    halo = 8
    hb = tb // halo
    n_h = t_len // halo
    c0 = col0 // tc
    kern = functools.partial(_conv_silu_kernel, tb=tb)
    return pl.pallas_call(
        kern,
        grid=(bsz, t_len // tb, ch // tc),
        in_specs=[pl.BlockSpec((None, halo, tc), lambda b, i, c: (b, jnp.maximum(i * hb - 1, 0), c0 + c)),
                  pl.BlockSpec((None, tb, tc), lambda b, i, c: (b, i, c0 + c)),
                  pl.BlockSpec((None, halo, tc), lambda b, i, c: (b, jnp.minimum((i + 1) * hb, n_h - 1), c0 + c)),
                  pl.BlockSpec((CONV_C, tc), lambda b, i, c: (0, c)),
                  pl.BlockSpec((1, tc), lambda b, i, c: (0, c))],
        out_specs=pl.BlockSpec((None, tb, tc), lambda b, i, c: (b, i, c)),
        out_shape=jax.ShapeDtypeStruct((bsz, t_len, ch), F32),
        scratch_shapes=[pltpu.VMEM((tb + 2 * halo, tc), F32)],
        compiler_params=_params("parallel", "parallel", "parallel"),
        name="conv_silu",
    )(z, z, z, conv_w, conv_b)


def _split_bf16(x, parts):
    pieces = []
    for _ in range(parts):
        p = x.astype(BF16)
        pieces.append(p)
        x = x - p.astype(F32)
    return jnp.concatenate(pieces, axis=1)


def _rep_matrix(heads, width, parts):
    r = jnp.repeat(jnp.eye(heads, dtype=BF16), width, axis=1)
    return jnp.concatenate([r] * parts, axis=0)


def _ssd_scan_kernel(x_ref, b_ref, c_ref, dtr_ref, dtb_ref, an_ref, rep_n_ref, rep_p_ref, h0_ref,
                     y_ref, hfin_ref, st_ref, *, reverse, groups):
    t = pl.program_id(1)
    cl = CHUNK_C
    heads = x_ref.shape[1] // P_C
    hpg = heads // groups
    lane0 = heads if reverse else 0

    @pl.when(t == 0)
    def _():
        st_ref[...] = h0_ref[...]

    dt2 = jnp.logaddexp(dtr_ref[...] + dtb_ref[...], 0.0)
    la2 = dt2 * (an_ref[...] * LOG2E)
    ri = lax.broadcasted_iota(jnp.int32, (cl, cl), 0)
    ci = lax.broadcasted_iota(jnp.int32, (cl, cl), 1)
    causal = (ci >= ri) if reverse else (ci <= ri)
    cum2 = jnp.dot(causal.astype(F32), la2, precision=lax.Precision.HIGHEST,
                   preferred_element_type=F32)
    cum_t = cum2.T[lane0:lane0 + heads, :]
    cum = cum2[:, lane0:lane0 + heads]
    dt = dt2[:, lane0:lane0 + heads]
    last = 0 if reverse else cl - 1
    cum_pieces = _split_bf16(cum, CUM_PARTS_C)
    dt_pieces = _split_bf16(dt, DT_PARTS_C)
    lane = lax.broadcasted_iota(jnp.int32, (cl, 2 * P_C), 1)
    lo = lane < P_C

    for g in range(groups):
        bm = b_ref[:, g * N_STATE_C:(g + 1) * N_STATE_C].astype(BF16)
        cm = c_ref[:, g * N_STATE_C:(g + 1) * N_STATE_C].astype(BF16)
        cb = _dot_nt(cm, bm)
        pairs = [g * hpg + 2 * p for p in range(hpg // 2)]
        xds, cums, outs = [], [], []
        for h_a in pairs:
            xs = slice(h_a * P_C, (h_a + 2) * P_C)
            ns = slice(h_a * N_STATE_C, (h_a + 2) * N_STATE_C)
            xds.append(x_ref[:, xs] * _dot(dt_pieces, rep_p_ref[:, xs]))
            cums.append(_dot(cum_pieces, rep_n_ref[:, ns]))
        for h_a, xd, cum_n in zip(pairs, xds, cums):
            xd_bf = xd.astype(BF16)
            ys = []
            for i in range(2):
                diff = cum_n[:, i * N_STATE_C:(i + 1) * N_STATE_C] - cum_t[h_a + i:h_a + i + 1, :]
                seg = jnp.exp2(jnp.where(causal, diff, -jnp.inf))
                ys.append(_dot((cb * seg).astype(BF16), xd_bf))
            outs.append(jnp.where(lo, ys[0], ys[1]))
        for h_a, xd, cum_n, y_intra in zip(pairs, xds, cums, outs):
            xs = slice(h_a * P_C, (h_a + 2) * P_C)
            cum_p = jnp.where(lo, cum_n[:, :N_STATE_C], cum_n[:, N_STATE_C:])
            tot_p = cum_p[last:last + 1, :]
            st = st_ref[h_a // 2]
            y_ref[:, xs] = y_intra + jnp.exp2(cum_p) * _dot(cm, st.astype(BF16))
            st_ref[h_a // 2] = (st * jnp.exp2(tot_p)
                                + _dot_tn(bm, (xd * jnp.exp2(tot_p - cum_p)).astype(BF16)))

    @pl.when(t == pl.num_programs(1) - 1)
    def _():
        hfin_ref[...] = st_ref[...]


def _ssd_scan(xbc, z, dt_bias, a_neg, h0, *, reverse, d_inner, groups, dt_col0):
    bsz, t_len, _ = xbc.shape
    n_pairs = h0.shape[1]
    heads = 2 * n_pairs
    nc = t_len // CHUNK_C
    gn = groups * N_STATE_C
    two_h = dt_bias.shape[1]
    tt = (lambda t: nc - 1 - t) if reverse else (lambda t: t)
    kern = functools.partial(_ssd_scan_kernel, reverse=reverse, groups=groups)
    state_spec = pl.BlockSpec((None, n_pairs, N_STATE_C, 2 * P_C), lambda b, t: (b, 0, 0, 0))
    rep_n = _rep_matrix(heads, N_STATE_C, CUM_PARTS_C)
    rep_p = _rep_matrix(heads, P_C, DT_PARTS_C)
    return pl.pallas_call(
        kern,
        grid=(bsz, nc),
        in_specs=[pl.BlockSpec((None, CHUNK_C, d_inner), lambda b, t: (b, tt(t), 0)),
                  pl.BlockSpec((None, CHUNK_C, gn), lambda b, t: (b, tt(t), d_inner // gn)),
                  pl.BlockSpec((None, CHUNK_C, gn), lambda b, t: (b, tt(t), d_inner // gn + 1)),
                  pl.BlockSpec((None, CHUNK_C, two_h), lambda b, t: (b, tt(t), dt_col0 // two_h)),
                  pl.BlockSpec((1, two_h), lambda b, t: (0, 0)),
                  pl.BlockSpec((1, two_h), lambda b, t: (0, 0)),
                  pl.BlockSpec(rep_n.shape, lambda b, t: (0, 0)),
                  pl.BlockSpec(rep_p.shape, lambda b, t: (0, 0)),
                  state_spec],
        out_specs=[pl.BlockSpec((None, CHUNK_C, d_inner), lambda b, t: (b, tt(t), 0)),
                   state_spec],
        out_shape=[jax.ShapeDtypeStruct((bsz, t_len, d_inner), F32),
                   jax.ShapeDtypeStruct(h0.shape, F32)],
        scratch_shapes=[pltpu.VMEM(h0.shape[1:], F32)],
        compiler_params=_params("parallel", "arbitrary"),
        name="ssd_scan_bwd" if reverse else "ssd_scan_fwd",
    )(xbc, xbc, xbc, z, dt_bias, a_neg, rep_n, rep_p, h0)


def _ssd_readout_kernel(yf_ref, yb_ref, x_ref, z_ref, d_ref, on_ref, a_ref, *, groups):
    gw = a_ref.shape[1] // groups
    for g in range(groups):
        gs = slice(g * gw, (g + 1) * gw)
        y = (yf_ref[:, gs] + yb_ref[:, gs] + d_ref[:, gs] * x_ref[:, gs]) * _silu(z_ref[:, gs])
        a_ref[:, gs] = (_rms(y) * on_ref[:, gs]).astype(BF16)


def _ssd_readout(y_f, y_b, xbc2d, z2d, d_exp, onorm, *, groups, tm=256):
    m, d_inner = y_f.shape
    tm = min(tm, m)
    kern = functools.partial(_ssd_readout_kernel, groups=groups)
    row = lambda i: (i, 0)
    return pl.pallas_call(
        kern,
        grid=(m // tm,),
        in_specs=[pl.BlockSpec((tm, d_inner), row),
                  pl.BlockSpec((tm, d_inner), row),
                  pl.BlockSpec((tm, d_inner), row),
                  pl.BlockSpec((tm, d_inner), row),
                  pl.BlockSpec((1, d_inner), lambda i: (0, 0)),
                  pl.BlockSpec((1, d_inner), lambda i: (0, 0))],
        out_specs=pl.BlockSpec((tm, d_inner), row),
        out_shape=jax.ShapeDtypeStruct((m, d_inner), BF16),
        compiler_params=_params("parallel"),
        name="ssd_readout",
    )(y_f, y_b, xbc2d, z2d, d_exp, onorm)


def _in_proj(h, mod, g4, w, *, rows_per_mod, tn):
    tm = 1024 if tn <= 1024 else 512
    return _modlinear(h, mod, g4, w, g_row=0, shift_row=0, scale_row=1,
                      rows_per_mod=rows_per_mod, tn=tn, tm=tm)


def _hgrn2_mixer(zc, zl, lb, onorm, emit_ctx):
    bsz, t_len, five_d = zl.shape
    d_a = five_d // 5
    heads = d_a // DK_A
    l_len = zc.shape[1]
    s0 = jnp.zeros((bsz, heads, DK_A, DK_A), F32)
    o_lat, o_ctx = [], []
    for d in range(2):
        lb_d = lb[d].reshape(1, d_a)
        oc, s_ctx = _hgrn2_scan(zc, lb_d, s0, reverse=bool(d))
        ol, _ = _hgrn2_scan(zl, lb_d, s_ctx, reverse=bool(d))
        o_ctx.append(oc)
        o_lat.append(ol)
    on = onorm.reshape(1, d_a)
    al = _hgrn2_readout(o_lat[0].reshape(bsz * t_len, d_a), o_lat[1].reshape(bsz * t_len, d_a),
                        zl.reshape(bsz * t_len, five_d), on)
    ac = None
    if emit_ctx:
        ac = _hgrn2_readout(o_ctx[0].reshape(bsz * l_len, d_a), o_ctx[1].reshape(bsz * l_len, d_a),
                            zc.reshape(bsz * l_len, five_d), on)
    return ac, al


def _rope_tables(t_len):
    rows = t_len // GRID_W
    inv_freq = ROPE_BASE ** (-jnp.arange(ROPE_PAIRS, dtype=F32) / ROPE_PAIRS)
    row = jnp.repeat(jnp.arange(rows, dtype=F32), GRID_W)
    col = jnp.tile(jnp.arange(GRID_W, dtype=F32), rows)
    ang_row = row[:, None] * inv_freq
    ang_col = col[:, None] * inv_freq
    cos = jnp.concatenate([jnp.cos(ang_row)] * 2 + [jnp.cos(ang_col)] * 2, axis=-1)
    sin = jnp.concatenate([-jnp.sin(ang_row), jnp.sin(ang_row),
                           -jnp.sin(ang_col), jnp.sin(ang_col)], axis=-1)
    return cos, sin


def _swa_mixer(zc, zl, sink, emit_ctx):
    bsz, t_len, w = zl.shape
    l_len = zc.shape[1]
    n_kv = sink.shape[0] // GRP_B
    n_q = sink.shape[0]
    cos, sin = _rope_tables(t_len)
    ql = _rope_cast(zl, cos, sin, n_q=n_q, n_rot=n_q + n_kv)
    ones = jnp.ones((l_len, DH_B), F32)
    qc = _rope_cast(zc, ones, ones, n_q=n_q, n_rot=0)
    sink2 = sink.astype(F32).reshape(n_kv, GRP_B)
    al = _swa_attention(sink2, ql, qc, n_q=n_q, n_kv=n_kv).reshape(bsz * t_len, n_q * DH_B)
    ac = None
    if emit_ctx:
        ac = _ctx_attention(sink2, qc, n_q=n_q, n_kv=n_kv).reshape(bsz * l_len, n_q * DH_B)
    return ac, al


def _ssd_mixer(zc, zl, conv_w, conv_b, dt_bias, a_log, d_skip, onorm, emit_ctx):
    bsz = zl.shape[0]
    d_inner = onorm.shape[0]
    heads = d_skip.shape[0]
    groups = N_GROUPS_C
    gn = groups * N_STATE_C
    a_neg = (-jnp.exp(a_log.astype(F32))).reshape(1, 2 * heads)
    dtb = dt_bias.astype(F32).reshape(1, 2 * heads)
    d_exp = jnp.repeat(d_skip.astype(F32), P_C).reshape(1, d_inner)
    on = onorm.reshape(1, d_inner)
    cb = conv_b.reshape(1, -1)
    xbc_c = _conv_silu(zc, conv_w, cb, col0=d_inner)
    xbc_l = _conv_silu(zl, conv_w, cb, col0=d_inner)
    h0 = jnp.zeros((bsz, heads // 2, N_STATE_C, 2 * P_C), F32)
    kw = dict(d_inner=d_inner, groups=groups, dt_col0=2 * d_inner + 2 * gn)
    y_lat, y_ctx = [], []
    for d in range(2):
        yc, h_ctx = _ssd_scan(xbc_c, zc, dtb, a_neg, h0, reverse=bool(d), **kw)
        yl, _ = _ssd_scan(xbc_l, zl, dtb, a_neg, h_ctx, reverse=bool(d), **kw)
        y_ctx.append(yc)
        y_lat.append(yl)

    def readout(ys, xbc, z):
        m = z.shape[0] * z.shape[1]
        return _ssd_readout(ys[0].reshape(m, d_inner), ys[1].reshape(m, d_inner),
                            xbc.reshape(m, -1), z.reshape(m, -1), d_exp, on, groups=groups)

    ac = readout(y_ctx, xbc_c, zc) if emit_ctx else None
    return ac, readout(y_lat, xbc_l, zl)


def kernel(x, c, ctx, c_ctx, ada_w, ada_b, norm_g, mlp_w1, mlp_w2, a_w_in, a_lb_logits, a_onorm,
           a_w_out, b_w_qkv, b_sink, b_w_out, c_w_in, c_conv_w, c_conv_b, c_dt_bias, c_a_log, c_d,
           c_onorm, c_w_out):
    bsz, t_len, d = x.shape
    l_len = ctx.shape[1]
    depth = ada_w.shape[0]

    p_lb = jax.nn.softmax(a_lb_logits.astype(F32), axis=0)
    lower_bounds = jnp.cumsum(p_lb, axis=0) - p_lb

    c_rows = jnp.concatenate([c, c_ctx[None, :], jnp.zeros((8 - bsz - 1, d), F32)], axis=0)
    mods = _ada(c_rows, ada_w, ada_b).reshape(depth, 8, 6, d)

    hl = x.reshape(bsz * t_len, d)
    hc = ctx.reshape(bsz * l_len, d)
    rows_l, rows_c = t_len, bsz * l_len
    for i in range(depth):
        emit_ctx = i < depth - 1
        m_l = mods[i, :bsz]
        m_c = mods[i, bsz:bsz + 1]
        g4 = norm_g[i]
        kind, j = i % N_MIXERS, i // N_MIXERS
        if kind == 0:
            w_in, w_out, tn = a_w_in[j].astype(BF16), a_w_out[j].astype(BF16), 1024
        elif kind == 1:
            w_in, w_out, tn = b_w_qkv[j].astype(BF16), b_w_out[j].astype(BF16), 1024
        else:
            w_in, w_out, tn = c_w_in[j].astype(BF16), c_w_out[j].astype(BF16), 1152
        zl = _in_proj(hl, m_l, g4, w_in, rows_per_mod=rows_l, tn=tn).reshape(bsz, t_len, -1)
        zc = _in_proj(hc, m_c, g4, w_in, rows_per_mod=rows_c, tn=tn).reshape(bsz, l_len, -1)
        if kind == 0:
            ac, al = _hgrn2_mixer(zc, zl, lower_bounds[i], a_onorm[j], emit_ctx)
        elif kind == 1:
            ac, al = _swa_mixer(zc, zl, b_sink[j], emit_ctx)
        else:
            ac, al = _ssd_mixer(zc, zl, c_conv_w[j], c_conv_b[j], c_dt_bias[j], c_a_log[j], c_d[j],
                                c_onorm[j], emit_ctx)
        w1 = mlp_w1[i].astype(BF16)
        w2 = mlp_w2[i].astype(BF16)
        hl = _linear_resnorm(al, w_out, hl, m_l, g4, g_row=1, gate_row=2, rows_per_mod=rows_l)
        hl = _mlp(hl, m_l, g4, w1, w2, rows_per_mod=rows_l)
        if emit_ctx:
            hc = _linear_resnorm(ac, w_out, hc, m_c, g4, g_row=1, gate_row=2, rows_per_mod=rows_c)
            hc = _mlp(hc, m_c, g4, w1, w2, rows_per_mod=rows_c)
    return hl.reshape(bsz, t_len, d)
```

```python
import functools
import math

import jax
import jax.numpy as jnp
from jax import lax
from jax.experimental import pallas as pl
from jax.experimental.pallas import tpu as pltpu

F32 = jnp.float32
BF16 = jnp.bfloat16

EPS = 1e-6
GRID_W = 64
ROPE_BASE = 10000.0
N_MIXERS = 3

DK_A = 128
BLOCK_A = 128
HEADS_PER_BLOCK_A = 8
SUBLANES = 8
LOG2E = math.log2(math.e)

DH_B = 128
GRP_B = 4
WINDOW_B = 128
BLK_B = 128
SCALE_B = DH_B ** -0.5
ROPE_PAIRS = DH_B // 4

P_C = 64
N_GROUPS_C = 8
N_STATE_C = 128
CONV_C = 5
CHUNK_C = 128
CUM_PARTS_C = 3
DT_PARTS_C = 2
GROUPS_PER_STAGE_C = 2

VMEM_LIMIT = 56 * 1024 * 1024


def _params(*sem):
    return pltpu.CompilerParams(dimension_semantics=sem, vmem_limit_bytes=VMEM_LIMIT)


def _rms(x):
    return x * lax.rsqrt(jnp.mean(x * x, axis=-1, keepdims=True) + EPS)


def _silu(x):
    return x * jax.nn.sigmoid(x)


def _dot(a, b):
    return jnp.dot(a, b, preferred_element_type=F32)


def _dot_nt(a, b):
    return lax.dot_general(a, b, (((1,), (1,)), ((), ())), preferred_element_type=F32)


def _dot_tn(a, b):
    return lax.dot_general(a, b, (((0,), (0,)), ((), ())), preferred_element_type=F32)


def _ada_kernel(c_ref, w_ref, b_ref, o_ref):
    s = _silu(c_ref[...])
    o_ref[...] = jnp.dot(s, w_ref[...], precision=lax.Precision.HIGHEST,
                         preferred_element_type=F32) + b_ref[...]


def _ada(c_rows, ada_w, ada_b, tn=768):
    depth, d, n = ada_w.shape
    rows = c_rows.shape[0]
    return pl.pallas_call(
        _ada_kernel,
        grid=(depth, n // tn),
        in_specs=[pl.BlockSpec((rows, d), lambda l, j: (0, 0)),
                  pl.BlockSpec((None, d, tn), lambda l, j: (l, 0, j)),
                  pl.BlockSpec((None, 1, tn), lambda l, j: (l, 0, j))],
        out_specs=pl.BlockSpec((None, rows, tn), lambda l, j: (l, 0, j)),
        out_shape=jax.ShapeDtypeStruct((depth, rows, n), F32),
        compiler_params=_params("parallel", "parallel"),
        name="ada_modulation",
    )(c_rows, ada_w, ada_b.reshape(depth, 1, n))


def _modlinear_kernel(h_ref, mod_ref, g_ref, w_ref, o_ref, u_ref, *, g_row, shift_row, scale_row):
    @pl.when(pl.program_id(1) == 0)
    def _():
        y = _rms(h_ref[...]) * g_ref[g_row:g_row + 1, :]
        u = y * (1.0 + mod_ref[scale_row:scale_row + 1, :]) + mod_ref[shift_row:shift_row + 1, :]
        u_ref[...] = u.astype(BF16)

    o_ref[...] = _dot(u_ref[...], w_ref[...]).astype(o_ref.dtype)


def _modlinear(h, mod, g4, w, *, g_row, shift_row, scale_row, rows_per_mod, tn, tm=1024):
    m, d = h.shape
    n = w.shape[1]
    tm = min(tm, m, rows_per_mod)
    tiles_per_mod = rows_per_mod // tm
    kern = functools.partial(_modlinear_kernel, g_row=g_row, shift_row=shift_row, scale_row=scale_row)
    return pl.pallas_call(
        kern,
        grid=(m // tm, n // tn),
        in_specs=[pl.BlockSpec((tm, d), lambda i, j: (i, 0)),
                  pl.BlockSpec((None, 6, d), lambda i, j: (i // tiles_per_mod, 0, 0)),
                  pl.BlockSpec((4, d), lambda i, j: (0, 0)),
                  pl.BlockSpec((d, tn), lambda i, j: (0, j))],
        out_specs=pl.BlockSpec((tm, tn), lambda i, j: (i, j)),
        out_shape=jax.ShapeDtypeStruct((m, n), F32),
        scratch_shapes=[pltpu.VMEM((tm, d), BF16)],
        compiler_params=_params("parallel", "arbitrary"),
        name="modulate_in_proj",
    )(h, mod, g4, w)


def _linear_resnorm_kernel(a_ref, w_ref, h_ref, mod_ref, g_ref, o_ref, acc_ref, *, g_row, gate_row):
    k = pl.program_id(1)

    @pl.when(k == 0)
    def _():
        acc_ref[...] = jnp.zeros_like(acc_ref)

    acc_ref[...] += _dot(a_ref[...], w_ref[...])

    @pl.when(k == pl.num_programs(1) - 1)
    def _():
        y = _rms(acc_ref[...]) * g_ref[g_row:g_row + 1, :]
        o_ref[...] = h_ref[...] + mod_ref[gate_row:gate_row + 1, :] * y


def _linear_resnorm(a, w, h, mod, g4, *, g_row, gate_row, rows_per_mod, tm=512, tk=2048):
    m, kdim = a.shape
    d = w.shape[1]
    tm = min(tm, m, rows_per_mod)
    tk = min(tk, kdim)
    tiles_per_mod = rows_per_mod // tm
    kern = functools.partial(_linear_resnorm_kernel, g_row=g_row, gate_row=gate_row)
    return pl.pallas_call(
        kern,
        grid=(m // tm, kdim // tk),
        in_specs=[pl.BlockSpec((tm, tk), lambda i, k: (i, k)),
                  pl.BlockSpec((tk, d), lambda i, k: (k, 0)),
                  pl.BlockSpec((tm, d), lambda i, k: (i, 0)),
                  pl.BlockSpec((None, 6, d), lambda i, k: (i // tiles_per_mod, 0, 0)),
                  pl.BlockSpec((4, d), lambda i, k: (0, 0))],
        out_specs=pl.BlockSpec((tm, d), lambda i, k: (i, 0)),
        out_shape=jax.ShapeDtypeStruct((m, d), F32),
        scratch_shapes=[pltpu.VMEM((tm, d), F32)],
        compiler_params=_params("parallel", "arbitrary"),
        name="out_proj_residual",
    )(a, w, h, mod, g4)


def _mlp_kernel(h_ref, mod_ref, g_ref, w1_ref, w2_ref, o_ref, u_ref, acc_ref):
    j = pl.program_id(1)

    @pl.when(j == 0)
    def _():
        y = _rms(h_ref[...]) * g_ref[2:3, :]
        u_ref[...] = (y * (1.0 + mod_ref[4:5, :]) + mod_ref[3:4, :]).astype(BF16)
        acc_ref[...] = jnp.zeros_like(acc_ref)

    a = jnp.maximum(_dot(u_ref[...], w1_ref[...]), 0.0)
    acc_ref[...] += _dot((a * a).astype(BF16), w2_ref[...])

    @pl.when(j == pl.num_programs(1) - 1)
    def _():
        y = _rms(acc_ref[...]) * g_ref[3:4, :]
        o_ref[...] = h_ref[...] + mod_ref[5:6, :] * y


def _mlp(h, mod, g4, w1, w2, *, rows_per_mod, tm=512, tf=1024):
    m, d = h.shape
    ff = w1.shape[1]
    tm = min(tm, m, rows_per_mod)
    tiles_per_mod = rows_per_mod // tm
    return pl.pallas_call(
        _mlp_kernel,
        grid=(m // tm, ff // tf),
        in_specs=[pl.BlockSpec((tm, d), lambda i, j: (i, 0)),
                  pl.BlockSpec((None, 6, d), lambda i, j: (i // tiles_per_mod, 0, 0)),
                  pl.BlockSpec((4, d), lambda i, j: (0, 0)),
                  pl.BlockSpec((d, tf), lambda i, j: (0, j)),
                  pl.BlockSpec((tf, d), lambda i, j: (j, 0))],
        out_specs=pl.BlockSpec((tm, d), lambda i, j: (i, 0)),
        out_shape=jax.ShapeDtypeStruct((m, d), F32),
        scratch_shapes=[pltpu.VMEM((tm, d), BF16), pltpu.VMEM((tm, d), F32)],
        compiler_params=_params("parallel", "arbitrary"),
        name="mlp_residual",
    )(h, mod, g4, w1, w2)


def _hgrn2_level_operand(m, q, k, g, lf, g_ref, gpad, row, reverse):
    query_half = 0 if reverse else 1
    r_off = m if reverse else m - 1

    def boundary(rows0, n):
        r = gpad + (rows0 // (2 * m)) * (2 * m) + r_off
        return jnp.broadcast_to(g_ref[r:r + 1, :], (n, DK_A))

    if m >= SUBLANES:
        pieces = []
        for j in range(BLOCK_A // m):
            rs = slice(j * m, (j + 1) * m)
            g_r = boundary(j * m, m)
            if j % 2 == query_half:
                pieces.append(q[rs] * jnp.exp2(g[rs] - g_r))
            else:
                pieces.append(k[rs] * jnp.exp2(g_r - g[rs]))
        return jnp.concatenate(pieces, axis=0).astype(BF16)

    is_query = ((row // m) % 2) == query_half
    if m == 1:
        return jnp.where(is_query, q * jnp.exp2(lf), k).astype(BF16)
    per_tile = SUBLANES // (2 * m)
    tiles = []
    for v in range(BLOCK_A // SUBLANES):
        g_r = boundary(v * SUBLANES, SUBLANES)
        for b in range(1, per_tile):
            in_b = ((row[0:SUBLANES] % SUBLANES) // (2 * m)) == b
            g_r = jnp.where(in_b, boundary(v * SUBLANES + b * 2 * m, SUBLANES), g_r)
        tiles.append(g_r)
    g_r = jnp.concatenate(tiles, axis=0)
    return (jnp.where(is_query, q, k) * jnp.exp2(-jnp.abs(g - g_r))).astype(BF16)


def _hgrn2_scan_kernel(zq_ref, zv_ref, zf_ref, lb_ref, s0_ref, o_ref, sfin_ref,
                       st_ref, q_s, k_s, lf_s, g_s, *, reverse, tb, hpb):
    t = pl.program_id(2)
    gpad = BLOCK_A // 2
    n_blocks = tb // BLOCK_A

    @pl.when(t == 0)
    def _():
        st_ref[...] = s0_ref[...]

    zero_gpad = jnp.zeros((gpad, DK_A), F32)
    for i in range(n_blocks * hpb):
        g_s[i, 0:gpad, :] = zero_gpad
        g_s[i, gpad + BLOCK_A:gpad + BLOCK_A + gpad, :] = zero_gpad

    lb = lb_ref[...]
    zf = zf_ref[...]
    q_s[...] = _silu(zq_ref[...])
    k_s[...] = (1.0 - lb) * jax.nn.sigmoid(-zf)
    lf_s[...] = LOG2E * jnp.logaddexp(jnp.log(lb), jnp.log1p(-lb) + jax.nn.log_sigmoid(zf))

    ti = lax.broadcasted_iota(jnp.int32, (BLOCK_A, BLOCK_A), 0)
    si = lax.broadcasted_iota(jnp.int32, (BLOCK_A, BLOCK_A), 1)
    levels = [BLOCK_A >> i for i in range(1, BLOCK_A.bit_length())]
    same = {m: (ti // m) == (si // m) for m in levels}
    kill = (si <= ti) if reverse else (si >= ti)
    sgn = 1 if reverse else -1
    last = 0 if reverse else BLOCK_A - 1

    block_order = range(n_blocks - 1, -1, -1) if reverse else range(n_blocks)
    units = [(blk, h) for blk in block_order for h in range(hpb)]
    view = {u: (slice(u[0] * BLOCK_A, (u[0] + 1) * BLOCK_A), slice(u[1] * DK_A, (u[1] + 1) * DK_A))
            for u in units}

    g = {u: lf_s[view[u]] for u in units}
    shift = 1
    while shift < BLOCK_A:
        for u in units:
            g_ref = g_s.at[u[0] * hpb + u[1]]
            g_ref[gpad:gpad + BLOCK_A, :] = g[u]
            off = gpad + sgn * shift
            g[u] = g[u] + g_ref[off:off + BLOCK_A, :]
        shift *= 2
    for u in units:
        g_s[u[0] * hpb + u[1], gpad:gpad + BLOCK_A, :] = g[u]

    attn = {}
    for m in levels:
        for u in units:
            g_ref = g_s.at[u[0] * hpb + u[1]]
            x = _hgrn2_level_operand(m, q_s[view[u]], k_s[view[u]], g[u], lf_s[view[u]], g_ref, gpad,
                                     ti, reverse)
            a = _dot_nt(x, x)
            attn[u] = jnp.where(same[2 * m], a, attn[u]) if u in attn else a

    for u in units:
        h = u[1]
        q = q_s[view[u]]
        k = k_s[view[u]]
        v = zv_ref[view[u]]
        v_bf = v.astype(BF16)
        glast = g[u][last:last + 1, :]
        qhat = (q * jnp.exp2(g[u])).astype(BF16)
        khat = (k * jnp.exp2(glast - g[u])).astype(BF16)
        st = st_ref[h]
        diag = jnp.sum(q * k, axis=-1, keepdims=True)
        o_ref[view[u]] = (diag * v + _dot(jnp.where(kill, 0.0, attn[u]).astype(BF16), v_bf)
                          + _dot_nt(qhat, st.astype(BF16)))
        st_ref[h] = st * jnp.exp2(glast) + _dot_tn(v_bf, khat)

    @pl.when(t == pl.num_programs(2) - 1)
    def _():
        sfin_ref[...] = st_ref[...]


def _hgrn2_scan(z, lb, s0, *, reverse, tb=256):
    bsz, t_len, _ = z.shape
    heads = s0.shape[1]
    hpb = HEADS_PER_BLOCK_A
    width = hpb * DK_A
    n_hb = heads // hpb
    tb = min(tb, t_len)
    nt = t_len // tb
    tt = (lambda t: nt - 1 - t) if reverse else (lambda t: t)
    f_base = (4 if reverse else 3) * n_hb
    kern = functools.partial(_hgrn2_scan_kernel, reverse=reverse, tb=tb, hpb=hpb)
    state_spec = pl.BlockSpec((None, hpb, DK_A, DK_A), lambda b, hb, t: (b, hb, 0, 0))
    return pl.pallas_call(
        kern,
        grid=(bsz, n_hb, nt),
        in_specs=[pl.BlockSpec((None, tb, width), lambda b, hb, t: (b, tt(t), hb)),
                  pl.BlockSpec((None, tb, width), lambda b, hb, t: (b, tt(t), 2 * n_hb + hb)),
                  pl.BlockSpec((None, tb, width), lambda b, hb, t: (b, tt(t), f_base + hb)),
                  pl.BlockSpec((1, width), lambda b, hb, t: (0, hb)),
                  state_spec],
        out_specs=[pl.BlockSpec((None, tb, width), lambda b, hb, t: (b, tt(t), hb)),
                   state_spec],
        out_shape=[jax.ShapeDtypeStruct((bsz, t_len, heads * DK_A), F32),
                   jax.ShapeDtypeStruct(s0.shape, F32)],
        scratch_shapes=[pltpu.VMEM((hpb, DK_A, DK_A), F32),
                        pltpu.VMEM((tb, width), F32),
                        pltpu.VMEM((tb, width), F32),
                        pltpu.VMEM((tb, width), F32),
                        pltpu.VMEM((tb // BLOCK_A * hpb, 2 * BLOCK_A, DK_A), F32)],
        compiler_params=_params("parallel", "parallel", "arbitrary"),
        name="hgrn2_scan_bwd" if reverse else "hgrn2_scan_fwd",
    )(z, z, z, lb, s0)


def _hgrn2_readout_kernel(of_ref, ob_ref, zg_ref, on_ref, a_ref, *, heads):
    for h in range(heads):
        hs = slice(h * DK_A, (h + 1) * DK_A)
        o = of_ref[:, hs] + ob_ref[:, hs]
        y = _rms(o) * on_ref[:, hs]
        a_ref[:, hs] = (y * _silu(zg_ref[:, hs])).astype(BF16)


def _hgrn2_readout(o_f, o_b, z2d, onorm, tm=512):
    m, d = o_f.shape
    tm = min(tm, m)
    kern = functools.partial(_hgrn2_readout_kernel, heads=d // DK_A)
    return pl.pallas_call(
        kern,
        grid=(m // tm,),
        in_specs=[pl.BlockSpec((tm, d), lambda i: (i, 0)),
                  pl.BlockSpec((tm, d), lambda i: (i, 0)),
                  pl.BlockSpec((tm, d), lambda i: (i, 1)),
                  pl.BlockSpec((1, d), lambda i: (0, 0))],
        out_specs=pl.BlockSpec((tm, d), lambda i: (i, 0)),
        out_shape=jax.ShapeDtypeStruct((m, d), BF16),
        compiler_params=_params("parallel"),
        name="hgrn2_readout",
    )(o_f, o_b, z2d, onorm)


def _rope_kernel(x_ref, cos_ref, sin_ref, o_ref, *, n_q, n_rot):
    cos = cos_ref[...]
    sin = sin_ref[...]
    lane = lax.broadcasted_iota(jnp.int32, cos.shape, 1)
    first_half = (lane % (DH_B // 2)) < ROPE_PAIRS
    for h in range(x_ref.shape[1] // DH_B):
        hs = slice(h * DH_B, (h + 1) * DH_B)
        x = x_ref[:, hs]
        if h < n_rot:
            partner = jnp.where(first_half, pltpu.roll(x, DH_B - ROPE_PAIRS, 1),
                                pltpu.roll(x, ROPE_PAIRS, 1))
            x = x * cos + partner * sin
        if h < n_q:
            x = x * SCALE_B
        o_ref[:, hs] = x.astype(BF16)


def _rope_cast(qkv, cos, sin, *, n_q, n_rot, tm=512):
    bsz, t_len, w = qkv.shape
    tm = min(tm, t_len)
    kern = functools.partial(_rope_kernel, n_q=n_q, n_rot=n_rot)
    return pl.pallas_call(
        kern,
        grid=(bsz, t_len // tm),
        in_specs=[pl.BlockSpec((None, tm, w), lambda b, i: (b, i, 0)),
                  pl.BlockSpec((tm, DH_B), lambda b, i: (i, 0)),
                  pl.BlockSpec((tm, DH_B), lambda b, i: (i, 0))],
        out_specs=pl.BlockSpec((None, tm, w), lambda b, i: (b, i, 0)),
        out_shape=jax.ShapeDtypeStruct((bsz, t_len, w), BF16),
        compiler_params=_params("parallel", "parallel"),
        name="rope_cast",
    )(qkv, cos, sin)


def _sink_column(sink_ref, kv, rows_per_head):
    return jnp.concatenate(
        [jnp.full((rows_per_head, 1), sink_ref[kv, g], F32) for g in range(GRP_B)], axis=0)


def _stack_heads(q):
    return jnp.concatenate([q[:, g * DH_B:(g + 1) * DH_B] for g in range(GRP_B)], axis=0)


def _with_ones(v):
    return jnp.concatenate([v, jnp.ones_like(v)], axis=1)


def _row_max(*scores):
    tiles = [s[:, t:t + DH_B] for s in scores for t in range(0, s.shape[1], DH_B)]
    return jnp.max(functools.reduce(jnp.maximum, tiles), axis=-1, keepdims=True)


def _swa_kernel(sink_ref, q_ref, kp_ref, kc_ref, kn_ref, vp_ref, vc_ref, vn_ref, kx_ref, vx_ref,
                o_ref, *, nq):
    kv = pl.program_id(1)
    j = pl.program_id(2)
    nj = pl.num_programs(2)
    kb = jnp.concatenate([kp_ref[...], kc_ref[...], kn_ref[...]], axis=0)
    vb = _with_ones(jnp.concatenate([vp_ref[...], vc_ref[...], vn_ref[...]], axis=0))
    kx = kx_ref[...]
    vx = _with_ones(vx_ref[...])
    sink = _sink_column(sink_ref, kv, BLK_B)
    shape = (GRP_B * BLK_B, 3 * BLK_B)
    r = lax.broadcasted_iota(jnp.int32, shape, 0) % BLK_B
    c = lax.broadcasted_iota(jnp.int32, shape, 1)
    in_window = jnp.abs(c - BLK_B - r) <= WINDOW_B
    for i in range(nq):
        rows = slice(i * BLK_B, (i + 1) * BLK_B)
        qs = _stack_heads(q_ref[rows, :])
        s_loc = _dot_nt(qs, kb[i * BLK_B:(i + 3) * BLK_B])
        valid = in_window
        if i == 0:
            valid = valid & ((c >= BLK_B) | (j > 0))
        if i == nq - 1:
            valid = valid & ((c < 2 * BLK_B) | (j < nj - 1))
        s_loc = jnp.where(valid, s_loc, -jnp.inf)
        s_ctx = _dot_nt(qs, kx)
        m = jnp.maximum(_row_max(s_loc, s_ctx), sink)
        p_loc = jnp.exp(s_loc - m)
        p_ctx = jnp.exp(s_ctx - m)
        acc = (_dot(p_loc.astype(BF16), vb[i * BLK_B:(i + 3) * BLK_B])
               + _dot(p_ctx.astype(BF16), vx))
        o = acc[:, :DH_B] / (acc[:, DH_B:DH_B + 1] + jnp.exp(sink - m))
        for g in range(GRP_B):
            o_ref[rows, g * DH_B:(g + 1) * DH_B] = o[g * BLK_B:(g + 1) * BLK_B].astype(BF16)


def _swa_attention(sink, qkv, qkv_ctx, *, n_q, n_kv):
    bsz, t_len, _ = qkv.shape
    l_len = qkv_ctx.shape[1]
    nb = t_len // BLK_B
    nq = math.gcd(nb, 4)
    gw = GRP_B * DH_B
    k0, v0 = n_q, n_q + n_kv

    def halo(col0, edge):
        return pl.BlockSpec(
            (None, BLK_B, DH_B),
            lambda b, kv, j: (b, jnp.clip(j * nq + edge, 0, nb - 1), col0 + kv))

    def cur(col0):
        return pl.BlockSpec((None, nq * BLK_B, DH_B), lambda b, kv, j: (b, j, col0 + kv))

    return pl.pallas_call(
        functools.partial(_swa_kernel, nq=nq),
        grid=(bsz, n_kv, nb // nq),
        in_specs=[pl.BlockSpec(memory_space=pltpu.SMEM),
                  pl.BlockSpec((None, nq * BLK_B, gw), lambda b, kv, j: (b, j, kv)),
                  halo(k0, -1), cur(k0), halo(k0, nq),
                  halo(v0, -1), cur(v0), halo(v0, nq),
                  pl.BlockSpec((None, l_len, DH_B), lambda b, kv, j: (b, 0, k0 + kv)),
                  pl.BlockSpec((None, l_len, DH_B), lambda b, kv, j: (b, 0, v0 + kv))],
        out_specs=pl.BlockSpec((None, nq * BLK_B, gw), lambda b, kv, j: (b, j, kv)),
        out_shape=jax.ShapeDtypeStruct((bsz, t_len, n_q * DH_B), BF16),
        compiler_params=_params("parallel", "parallel", "parallel"),
        name="swa_attention",
    )(sink, qkv, qkv, qkv, qkv, qkv, qkv, qkv, qkv_ctx, qkv_ctx)


def _ctx_attention_kernel(sink_ref, q_ref, kx_ref, vx_ref, o_ref):
    kv = pl.program_id(1)
    l_len = q_ref.shape[0]
    qs = _stack_heads(q_ref[...])
    s = _dot_nt(qs, kx_ref[...])
    sink = _sink_column(sink_ref, kv, l_len)
    m = jnp.maximum(jnp.max(s, axis=-1, keepdims=True), sink)
    p = jnp.exp(s - m)
    den = jnp.sum(p, axis=-1, keepdims=True) + jnp.exp(sink - m)
    o = _dot(p.astype(BF16), vx_ref[...]) / den
    for g in range(GRP_B):
        o_ref[:, g * DH_B:(g + 1) * DH_B] = o[g * l_len:(g + 1) * l_len].astype(BF16)


def _ctx_attention(sink, qkv_ctx, *, n_q, n_kv):
    bsz, l_len, _ = qkv_ctx.shape
    gw = GRP_B * DH_B
    return pl.pallas_call(
        _ctx_attention_kernel,
        grid=(bsz, n_kv),
        in_specs=[pl.BlockSpec(memory_space=pltpu.SMEM),
                  pl.BlockSpec((None, l_len, gw), lambda b, kv: (b, 0, kv)),
                  pl.BlockSpec((None, l_len, DH_B), lambda b, kv: (b, 0, n_q + kv)),
                  pl.BlockSpec((None, l_len, DH_B), lambda b, kv: (b, 0, n_q + n_kv + kv))],
        out_specs=pl.BlockSpec((None, l_len, gw), lambda b, kv: (b, 0, kv)),
        out_shape=jax.ShapeDtypeStruct((bsz, l_len, n_q * DH_B), BF16),
        compiler_params=_params("parallel", "parallel"),
        name="ctx_attention",
    )(sink, qkv_ctx, qkv_ctx, qkv_ctx)


def _conv_silu_kernel(prev_ref, cur_ref, next_ref, w_ref, b_ref, o_ref, ext_ref, *, tb):
    i = pl.program_id(1)
    halo = 8
    has_prev = (i > 0).astype(F32)
    has_next = (i < pl.num_programs(1) - 1).astype(F32)
    ext_ref[0:halo, :] = prev_ref[...] * has_prev
    ext_ref[halo:halo + tb, :] = cur_ref[...]
    ext_ref[halo + tb:halo + tb + halo, :] = next_ref[...] * has_next
    acc = jnp.zeros(o_ref.shape, F32) + b_ref[...]
    for tap in range(CONV_C):
        off = halo + tap - CONV_C // 2
        acc = acc + ext_ref[off:off + tb, :] * w_ref[tap:tap + 1, :]
    o_ref[...] = _silu(acc)


def _conv_silu(z, conv_w, conv_b, *, col0, tb=512, tc=1024):
    bsz, t_len, _ = z.shape
    ch = conv_w.shape[1]
    tb = min(tb, t_len)
    halo = 8
    hb = tb // halo
    n_h = t_len // halo
    c0 = col0 // tc
    kern = functools.partial(_conv_silu_kernel, tb=tb)
    return pl.pallas_call(
        kern,
        grid=(bsz, t_len // tb, ch // tc),
        in_specs=[pl.BlockSpec((None, halo, tc), lambda b, i, c: (b, jnp.maximum(i * hb - 1, 0), c0 + c)),
                  pl.BlockSpec((None, tb, tc), lambda b, i, c: (b, i, c0 + c)),
                  pl.BlockSpec((None, halo, tc), lambda b, i, c: (b, jnp.minimum((i + 1) * hb, n_h - 1), c0 + c)),
                  pl.BlockSpec((CONV_C, tc), lambda b, i, c: (0, c)),
                  pl.BlockSpec((1, tc), lambda b, i, c: (0, c))],
        out_specs=pl.BlockSpec((None, tb, tc), lambda b, i, c: (b, i, c)),
        out_shape=jax.ShapeDtypeStruct((bsz, t_len, ch), F32),
        scratch_shapes=[pltpu.VMEM((tb + 2 * halo, tc), F32)],
        compiler_params=_params("parallel", "parallel", "parallel"),
        name="conv_silu",
    )(z, z, z, conv_w, conv_b)


def _split_bf16(x, parts):
    pieces = []
    for _ in range(parts):
        p = x.astype(BF16)
        pieces.append(p)
        x = x - p.astype(F32)
    return jnp.concatenate(pieces, axis=1)


def _rep_matrix(heads, width, parts):
    r = jnp.repeat(jnp.eye(heads, dtype=BF16), width, axis=1)
    return jnp.concatenate([r] * parts, axis=0)


def _ssd_scan_kernel(x_ref, b_ref, c_ref, dtr_ref, dtb_ref, an_ref, rep_n_ref, rep_p_ref, h0_ref,
                     y_ref, hfin_ref, st_ref, *, reverse, groups):
    t = pl.program_id(1)
    cl = CHUNK_C
    heads = x_ref.shape[1] // P_C
    hpg = heads // groups
    lane0 = heads if reverse else 0

    @pl.when(t == 0)
    def _():
        st_ref[...] = h0_ref[...]

    dt2 = jnp.logaddexp(dtr_ref[...] + dtb_ref[...], 0.0)
    la2 = dt2 * (an_ref[...] * LOG2E)
    ri = lax.broadcasted_iota(jnp.int32, (cl, cl), 0)
    ci = lax.broadcasted_iota(jnp.int32, (cl, cl), 1)
    causal = (ci >= ri) if reverse else (ci <= ri)
    cum2 = jnp.dot(causal.astype(F32), la2, precision=lax.Precision.HIGHEST,
                   preferred_element_type=F32)
    cum_t = cum2.T[lane0:lane0 + heads, :]
    cum = cum2[:, lane0:lane0 + heads]
    dt = dt2[:, lane0:lane0 + heads]
    last = 0 if reverse else cl - 1
    cum_pieces = _split_bf16(cum, CUM_PARTS_C)
    dt_pieces = _split_bf16(dt, DT_PARTS_C)
    lane = lax.broadcasted_iota(jnp.int32, (cl, 2 * P_C), 1)
    lo = lane < P_C

    for g0 in range(0, groups, GROUPS_PER_STAGE_C):
        stage_groups = range(g0, g0 + GROUPS_PER_STAGE_C)
        bm = {g: b_ref[:, g * N_STATE_C:(g + 1) * N_STATE_C].astype(BF16) for g in stage_groups}
        cm = {g: c_ref[:, g * N_STATE_C:(g + 1) * N_STATE_C].astype(BF16) for g in stage_groups}
        cb = {g: _dot_nt(cm[g], bm[g]) for g in stage_groups}
        pairs = [(g, g * hpg + 2 * p) for g in stage_groups for p in range(hpg // 2)]
        xds, cums, outs = [], [], []
        for _, h_a in pairs:
            xs = slice(h_a * P_C, (h_a + 2) * P_C)
            ns = slice(h_a * N_STATE_C, (h_a + 2) * N_STATE_C)
            xds.append(x_ref[:, xs] * _dot(dt_pieces, rep_p_ref[:, xs]))
            cums.append(_dot(cum_pieces, rep_n_ref[:, ns]))
        for (g, h_a), xd, cum_n in zip(pairs, xds, cums):
            xd_bf = xd.astype(BF16)
            ys = []
            for i in range(2):
                diff = cum_n[:, i * N_STATE_C:(i + 1) * N_STATE_C] - cum_t[h_a + i:h_a + i + 1, :]
                seg = jnp.exp2(jnp.where(causal, diff, -jnp.inf))
                ys.append(_dot((cb[g] * seg).astype(BF16), xd_bf))
            outs.append(jnp.where(lo, ys[0], ys[1]))
        for (g, h_a), xd, cum_n, y_intra in zip(pairs, xds, cums, outs):
            xs = slice(h_a * P_C, (h_a + 2) * P_C)
            cum_p = jnp.where(lo, cum_n[:, :N_STATE_C], cum_n[:, N_STATE_C:])
            tot_p = cum_p[last:last + 1, :]
            st = st_ref[h_a // 2]
            y_ref[:, xs] = y_intra + jnp.exp2(cum_p) * _dot(cm[g], st.astype(BF16))
            st_ref[h_a // 2] = (st * jnp.exp2(tot_p)
                                + _dot_tn(bm[g], (xd * jnp.exp2(tot_p - cum_p)).astype(BF16)))

    @pl.when(t == pl.num_programs(1) - 1)
    def _():
        hfin_ref[...] = st_ref[...]


def _ssd_scan(xbc, z, dt_bias, a_neg, h0, *, reverse, d_inner, groups, dt_col0):
    bsz, t_len, _ = xbc.shape
    n_pairs = h0.shape[1]
    heads = 2 * n_pairs
    nc = t_len // CHUNK_C
    gn = groups * N_STATE_C
    two_h = dt_bias.shape[1]
    tt = (lambda t: nc - 1 - t) if reverse else (lambda t: t)
    kern = functools.partial(_ssd_scan_kernel, reverse=reverse, groups=groups)
    state_spec = pl.BlockSpec((None, n_pairs, N_STATE_C, 2 * P_C), lambda b, t: (b, 0, 0, 0))
    rep_n = _rep_matrix(heads, N_STATE_C, CUM_PARTS_C)
    rep_p = _rep_matrix(heads, P_C, DT_PARTS_C)
    return pl.pallas_call(
        kern,
        grid=(bsz, nc),
        in_specs=[pl.BlockSpec((None, CHUNK_C, d_inner), lambda b, t: (b, tt(t), 0)),
                  pl.BlockSpec((None, CHUNK_C, gn), lambda b, t: (b, tt(t), d_inner // gn)),
                  pl.BlockSpec((None, CHUNK_C, gn), lambda b, t: (b, tt(t), d_inner // gn + 1)),
                  pl.BlockSpec((None, CHUNK_C, two_h), lambda b, t: (b, tt(t), dt_col0 // two_h)),
                  pl.BlockSpec((1, two_h), lambda b, t: (0, 0)),
                  pl.BlockSpec((1, two_h), lambda b, t: (0, 0)),
                  pl.BlockSpec(rep_n.shape, lambda b, t: (0, 0)),
                  pl.BlockSpec(rep_p.shape, lambda b, t: (0, 0)),
                  state_spec],
        out_specs=[pl.BlockSpec((None, CHUNK_C, d_inner), lambda b, t: (b, tt(t), 0)),
                   state_spec],
        out_shape=[jax.ShapeDtypeStruct((bsz, t_len, d_inner), F32),
                   jax.ShapeDtypeStruct(h0.shape, F32)],
        scratch_shapes=[pltpu.VMEM(h0.shape[1:], F32)],
        compiler_params=_params("parallel", "arbitrary"),
        name="ssd_scan_bwd" if reverse else "ssd_scan_fwd",
    )(xbc, xbc, xbc, z, dt_bias, a_neg, rep_n, rep_p, h0)


def _ssd_readout_kernel(yf_ref, yb_ref, x_ref, z_ref, d_ref, on_ref, a_ref, *, groups):
    gw = a_ref.shape[1] // groups
    for g in range(groups):
        gs = slice(g * gw, (g + 1) * gw)
        y = (yf_ref[:, gs] + yb_ref[:, gs] + d_ref[:, gs] * x_ref[:, gs]) * _silu(z_ref[:, gs])
        a_ref[:, gs] = (_rms(y) * on_ref[:, gs]).astype(BF16)


def _ssd_readout(y_f, y_b, xbc2d, z2d, d_exp, onorm, *, groups, tm=256):
    m, d_inner = y_f.shape
    tm = min(tm, m)
    kern = functools.partial(_ssd_readout_kernel, groups=groups)
    row = lambda i: (i, 0)
    return pl.pallas_call(
        kern,
        grid=(m // tm,),
        in_specs=[pl.BlockSpec((tm, d_inner), row),
                  pl.BlockSpec((tm, d_inner), row),
                  pl.BlockSpec((tm, d_inner), row),
                  pl.BlockSpec((tm, d_inner), row),
                  pl.BlockSpec((1, d_inner), lambda i: (0, 0)),
                  pl.BlockSpec((1, d_inner), lambda i: (0, 0))],
        out_specs=pl.BlockSpec((tm, d_inner), row),
        out_shape=jax.ShapeDtypeStruct((m, d_inner), BF16),
        compiler_params=_params("parallel"),
        name="ssd_readout",
    )(y_f, y_b, xbc2d, z2d, d_exp, onorm)


def _in_proj(h, mod, g4, w, *, rows_per_mod, tn):
    return _modlinear(h, mod, g4, w, g_row=0, shift_row=0, scale_row=1,
                      rows_per_mod=rows_per_mod, tn=tn)


def _hgrn2_mixer(zc, zl, lb, onorm, emit_ctx):
    bsz, t_len, five_d = zl.shape
    d_a = five_d // 5
    heads = d_a // DK_A
    l_len = zc.shape[1]
    s0 = jnp.zeros((bsz, heads, DK_A, DK_A), F32)
    o_lat, o_ctx = [], []
    for d in range(2):
        lb_d = lb[d].reshape(1, d_a)
        oc, s_ctx = _hgrn2_scan(zc, lb_d, s0, reverse=bool(d))
        ol, _ = _hgrn2_scan(zl, lb_d, s_ctx, reverse=bool(d))
        o_ctx.append(oc)
        o_lat.append(ol)
    on = onorm.reshape(1, d_a)
    al = _hgrn2_readout(o_lat[0].reshape(bsz * t_len, d_a), o_lat[1].reshape(bsz * t_len, d_a),
                        zl.reshape(bsz * t_len, five_d), on)
    ac = None
    if emit_ctx:
        ac = _hgrn2_readout(o_ctx[0].reshape(bsz * l_len, d_a), o_ctx[1].reshape(bsz * l_len, d_a),
                            zc.reshape(bsz * l_len, five_d), on)
    return ac, al


def _rope_tables(t_len):
    rows = t_len // GRID_W
    inv_freq = ROPE_BASE ** (-jnp.arange(ROPE_PAIRS, dtype=F32) / ROPE_PAIRS)
    row = jnp.repeat(jnp.arange(rows, dtype=F32), GRID_W)
    col = jnp.tile(jnp.arange(GRID_W, dtype=F32), rows)
    ang_row = row[:, None] * inv_freq
    ang_col = col[:, None] * inv_freq
    cos = jnp.concatenate([jnp.cos(ang_row)] * 2 + [jnp.cos(ang_col)] * 2, axis=-1)
    sin = jnp.concatenate([-jnp.sin(ang_row), jnp.sin(ang_row),
                           -jnp.sin(ang_col), jnp.sin(ang_col)], axis=-1)
    return cos, sin


def _swa_mixer(zc, zl, sink, emit_ctx):
    bsz, t_len, w = zl.shape
    l_len = zc.shape[1]
    n_kv = sink.shape[0] // GRP_B
    n_q = sink.shape[0]
    cos, sin = _rope_tables(t_len)
    ql = _rope_cast(zl, cos, sin, n_q=n_q, n_rot=n_q + n_kv)
    ones = jnp.ones((l_len, DH_B), F32)
    qc = _rope_cast(zc, ones, ones, n_q=n_q, n_rot=0)
    sink2 = sink.astype(F32).reshape(n_kv, GRP_B)
    al = _swa_attention(sink2, ql, qc, n_q=n_q, n_kv=n_kv).reshape(bsz * t_len, n_q * DH_B)
    ac = None
    if emit_ctx:
        ac = _ctx_attention(sink2, qc, n_q=n_q, n_kv=n_kv).reshape(bsz * l_len, n_q * DH_B)
    return ac, al


def _ssd_mixer(zc, zl, conv_w, conv_b, dt_bias, a_log, d_skip, onorm, emit_ctx):
    bsz = zl.shape[0]
    d_inner = onorm.shape[0]
    heads = d_skip.shape[0]
    groups = N_GROUPS_C
    gn = groups * N_STATE_C
    a_neg = (-jnp.exp(a_log.astype(F32))).reshape(1, 2 * heads)
    dtb = dt_bias.astype(F32).reshape(1, 2 * heads)
    d_exp = jnp.repeat(d_skip.astype(F32), P_C).reshape(1, d_inner)
    on = onorm.reshape(1, d_inner)
    cb = conv_b.reshape(1, -1)
    xbc_c = _conv_silu(zc, conv_w, cb, col0=d_inner)
    xbc_l = _conv_silu(zl, conv_w, cb, col0=d_inner)
    h0 = jnp.zeros((bsz, heads // 2, N_STATE_C, 2 * P_C), F32)
    kw = dict(d_inner=d_inner, groups=groups, dt_col0=2 * d_inner + 2 * gn)
    y_lat, y_ctx = [], []
    for d in range(2):
        yc, h_ctx = _ssd_scan(xbc_c, zc, dtb, a_neg, h0, reverse=bool(d), **kw)
        yl, _ = _ssd_scan(xbc_l, zl, dtb, a_neg, h_ctx, reverse=bool(d), **kw)
        y_ctx.append(yc)
        y_lat.append(yl)

    def readout(ys, xbc, z):
        m = z.shape[0] * z.shape[1]
        return _ssd_readout(ys[0].reshape(m, d_inner), ys[1].reshape(m, d_inner),
                            xbc.reshape(m, -1), z.reshape(m, -1), d_exp, on, groups=groups)

    ac = readout(y_ctx, xbc_c, zc) if emit_ctx else None
    return ac, readout(y_lat, xbc_l, zl)


def kernel(x, c, ctx, c_ctx, ada_w, ada_b, norm_g, mlp_w1, mlp_w2, a_w_in, a_lb_logits, a_onorm,
           a_w_out, b_w_qkv, b_sink, b_w_out, c_w_in, c_conv_w, c_conv_b, c_dt_bias, c_a_log, c_d,
           c_onorm, c_w_out):
    bsz, t_len, d = x.shape
    l_len = ctx.shape[1]
    depth = ada_w.shape[0]

    p_lb = jax.nn.softmax(a_lb_logits.astype(F32), axis=0)
    lower_bounds = jnp.cumsum(p_lb, axis=0) - p_lb

    c_rows = jnp.concatenate([c, c_ctx[None, :], jnp.zeros((8 - bsz - 1, d), F32)], axis=0)
    mods = _ada(c_rows, ada_w, ada_b).reshape(depth, 8, 6, d)

    hl = x.reshape(bsz * t_len, d)
    hc = ctx.reshape(bsz * l_len, d)
    rows_l, rows_c = t_len, bsz * l_len
    for i in range(depth):
        emit_ctx = i < depth - 1
        m_l = mods[i, :bsz]
        m_c = mods[i, bsz:bsz + 1]
        g4 = norm_g[i]
        kind, j = i % N_MIXERS, i // N_MIXERS
        if kind == 0:
            w_in, w_out, tn = a_w_in[j].astype(BF16), a_w_out[j].astype(BF16), 1024
        elif kind == 1:
            w_in, w_out, tn = b_w_qkv[j].astype(BF16), b_w_out[j].astype(BF16), 1024
        else:
            w_in, w_out, tn = c_w_in[j].astype(BF16), c_w_out[j].astype(BF16), 1152
        zl = _in_proj(hl, m_l, g4, w_in, rows_per_mod=rows_l, tn=tn).reshape(bsz, t_len, -1)
        zc = _in_proj(hc, m_c, g4, w_in, rows_per_mod=rows_c, tn=tn).reshape(bsz, l_len, -1)
        if kind == 0:
            ac, al = _hgrn2_mixer(zc, zl, lower_bounds[i], a_onorm[j], emit_ctx)
        elif kind == 1:
            ac, al = _swa_mixer(zc, zl, b_sink[j], emit_ctx)
        else:
            ac, al = _ssd_mixer(zc, zl, c_conv_w[j], c_conv_b[j], c_dt_bias[j], c_a_log[j], c_d[j],
                                c_onorm[j], emit_ctx)
        w1 = mlp_w1[i].astype(BF16)
        w2 = mlp_w2[i].astype(BF16)
        hl = _linear_resnorm(al, w_out, hl, m_l, g4, g_row=1, gate_row=2, rows_per_mod=rows_l)
        hl = _mlp(hl, m_l, g4, w1, w2, rows_per_mod=rows_l)
        if emit_ctx:
            hc = _linear_resnorm(ac, w_out, hc, m_c, g4, g_row=1, gate_row=2, rows_per_mod=rows_c)
            hc = _mlp(hc, m_c, g4, w1, w2, rows_per_mod=rows_c)
    return hl.reshape(bsz, t_len, d)
```

```python
import functools
import math

import jax
import jax.numpy as jnp
from jax import lax
from jax.experimental import pallas as pl
from jax.experimental.pallas import tpu as pltpu

F32 = jnp.float32
BF16 = jnp.bfloat16

EPS = 1e-6
GRID_W = 64
ROPE_BASE = 10000.0
N_MIXERS = 3

DK_A = 128
BLOCK_A = 128
HEADS_PER_BLOCK_A = 8
SUBLANES = 8
LOG2E = math.log2(math.e)

DH_B = 128
GRP_B = 4
WINDOW_B = 128
BLK_B = 128
SCALE_B = DH_B ** -0.5
ROPE_PAIRS = DH_B // 4

P_C = 64
N_GROUPS_C = 8
N_STATE_C = 128
CONV_C = 5
CHUNK_C = 128
CUM_PARTS_C = 3
DT_PARTS_C = 2
GROUPS_PER_STAGE_C = 2

VMEM_LIMIT = 56 * 1024 * 1024


def _params(*sem):
    return pltpu.CompilerParams(dimension_semantics=sem, vmem_limit_bytes=VMEM_LIMIT)


def _rms(x):
    return x * lax.rsqrt(jnp.mean(x * x, axis=-1, keepdims=True) + EPS)


def _silu(x):
    return x * jax.nn.sigmoid(x)


def _dot(a, b):
    return jnp.dot(a, b, preferred_element_type=F32)


def _dot_nt(a, b):
    return lax.dot_general(a, b, (((1,), (1,)), ((), ())), preferred_element_type=F32)


def _dot_tn(a, b):
    return lax.dot_general(a, b, (((0,), (0,)), ((), ())), preferred_element_type=F32)


def _ada_kernel(c_ref, w_ref, b_ref, o_ref):
    s = _silu(c_ref[...])
    o_ref[...] = jnp.dot(s, w_ref[...], precision=lax.Precision.HIGHEST,
                         preferred_element_type=F32) + b_ref[...]


def _ada(c_rows, ada_w, ada_b, tn=768):
    depth, d, n = ada_w.shape
    rows = c_rows.shape[0]
    return pl.pallas_call(
        _ada_kernel,
        grid=(depth, n // tn),
        in_specs=[pl.BlockSpec((rows, d), lambda l, j: (0, 0)),
                  pl.BlockSpec((None, d, tn), lambda l, j: (l, 0, j)),
                  pl.BlockSpec((None, 1, tn), lambda l, j: (l, 0, j))],
        out_specs=pl.BlockSpec((None, rows, tn), lambda l, j: (l, 0, j)),
        out_shape=jax.ShapeDtypeStruct((depth, rows, n), F32),
        compiler_params=_params("parallel", "parallel"),
        name="ada_modulation",
    )(c_rows, ada_w, ada_b.reshape(depth, 1, n))


def _modlinear_kernel(h_ref, mod_ref, g_ref, w_ref, o_ref, u_ref, *, g_row, shift_row, scale_row):
    @pl.when(pl.program_id(1) == 0)
    def _():
        y = _rms(h_ref[...]) * g_ref[g_row:g_row + 1, :]
        u = y * (1.0 + mod_ref[scale_row:scale_row + 1, :]) + mod_ref[shift_row:shift_row + 1, :]
        u_ref[...] = u.astype(BF16)

    o_ref[...] = _dot(u_ref[...], w_ref[...]).astype(o_ref.dtype)


def _modlinear(h, mod, g4, w, *, g_row, shift_row, scale_row, rows_per_mod, tn, tm=1024):
    m, d = h.shape
    n = w.shape[1]
    tm = min(tm, m, rows_per_mod)
    tiles_per_mod = rows_per_mod // tm
    kern = functools.partial(_modlinear_kernel, g_row=g_row, shift_row=shift_row, scale_row=scale_row)
    return pl.pallas_call(
        kern,
        grid=(m // tm, n // tn),
        in_specs=[pl.BlockSpec((tm, d), lambda i, j: (i, 0)),
                  pl.BlockSpec((None, 6, d), lambda i, j: (i // tiles_per_mod, 0, 0)),
                  pl.BlockSpec((4, d), lambda i, j: (0, 0)),
                  pl.BlockSpec((d, tn), lambda i, j: (0, j))],
        out_specs=pl.BlockSpec((tm, tn), lambda i, j: (i, j)),
        out_shape=jax.ShapeDtypeStruct((m, n), F32),
        scratch_shapes=[pltpu.VMEM((tm, d), BF16)],
        compiler_params=_params("parallel", "arbitrary"),
        name="modulate_in_proj",
    )(h, mod, g4, w)


def _linear_resnorm_kernel(a_ref, w_ref, h_ref, mod_ref, g_ref, o_ref, acc_ref, *, g_row, gate_row):
    k = pl.program_id(1)

    @pl.when(k == 0)
    def _():
        acc_ref[...] = jnp.zeros_like(acc_ref)

    acc_ref[...] += _dot(a_ref[...], w_ref[...])

    @pl.when(k == pl.num_programs(1) - 1)
    def _():
        y = _rms(acc_ref[...]) * g_ref[g_row:g_row + 1, :]
        o_ref[...] = h_ref[...] + mod_ref[gate_row:gate_row + 1, :] * y


def _linear_resnorm(a, w, h, mod, g4, *, g_row, gate_row, rows_per_mod, tm=512, tk=2048):
    m, kdim = a.shape
    d = w.shape[1]
    tm = min(tm, m, rows_per_mod)
    tk = min(tk, kdim)
    tiles_per_mod = rows_per_mod // tm
    kern = functools.partial(_linear_resnorm_kernel, g_row=g_row, gate_row=gate_row)
    return pl.pallas_call(
        kern,
        grid=(m // tm, kdim // tk),
        in_specs=[pl.BlockSpec((tm, tk), lambda i, k: (i, k)),
                  pl.BlockSpec((tk, d), lambda i, k: (k, 0)),
                  pl.BlockSpec((tm, d), lambda i, k: (i, 0)),
                  pl.BlockSpec((None, 6, d), lambda i, k: (i // tiles_per_mod, 0, 0)),
                  pl.BlockSpec((4, d), lambda i, k: (0, 0))],
        out_specs=pl.BlockSpec((tm, d), lambda i, k: (i, 0)),
        out_shape=jax.ShapeDtypeStruct((m, d), F32),
        scratch_shapes=[pltpu.VMEM((tm, d), F32)],
        compiler_params=_params("parallel", "arbitrary"),
        name="out_proj_residual",
    )(a, w, h, mod, g4)


def _mlp_kernel(h_ref, mod_ref, g_ref, w1_ref, w2_ref, o_ref, u_ref, acc_ref):
    j = pl.program_id(1)

    @pl.when(j == 0)
    def _():
        y = _rms(h_ref[...]) * g_ref[2:3, :]
        u_ref[...] = (y * (1.0 + mod_ref[4:5, :]) + mod_ref[3:4, :]).astype(BF16)
        acc_ref[...] = jnp.zeros_like(acc_ref)

    a = jnp.maximum(_dot(u_ref[...], w1_ref[...]), 0.0)
    acc_ref[...] += _dot((a * a).astype(BF16), w2_ref[...])

    @pl.when(j == pl.num_programs(1) - 1)
    def _():
        y = _rms(acc_ref[...]) * g_ref[3:4, :]
        o_ref[...] = h_ref[...] + mod_ref[5:6, :] * y


def _mlp(h, mod, g4, w1, w2, *, rows_per_mod, tm=512, tf=1024):
    m, d = h.shape
    ff = w1.shape[1]
    tm = min(tm, m, rows_per_mod)
    tiles_per_mod = rows_per_mod // tm
    return pl.pallas_call(
        _mlp_kernel,
        grid=(m // tm, ff // tf),
        in_specs=[pl.BlockSpec((tm, d), lambda i, j: (i, 0)),
                  pl.BlockSpec((None, 6, d), lambda i, j: (i // tiles_per_mod, 0, 0)),
                  pl.BlockSpec((4, d), lambda i, j: (0, 0)),
                  pl.BlockSpec((d, tf), lambda i, j: (0, j)),
                  pl.BlockSpec((tf, d), lambda i, j: (j, 0))],
        out_specs=pl.BlockSpec((tm, d), lambda i, j: (i, 0)),
        out_shape=jax.ShapeDtypeStruct((m, d), F32),
        scratch_shapes=[pltpu.VMEM((tm, d), BF16), pltpu.VMEM((tm, d), F32)],
        compiler_params=_params("parallel", "arbitrary"),
        name="mlp_residual",
    )(h, mod, g4, w1, w2)


def _hgrn2_level_operand(m, q, k, g, lf, g_ref, gpad, row, reverse):
    query_half = 0 if reverse else 1
    r_off = m if reverse else m - 1

    def boundary(rows0, n):
        r = gpad + (rows0 // (2 * m)) * (2 * m) + r_off
        return jnp.broadcast_to(g_ref[r:r + 1, :], (n, DK_A))

    if m >= SUBLANES:
        pieces = []
        for j in range(BLOCK_A // m):
            rs = slice(j * m, (j + 1) * m)
            g_r = boundary(j * m, m)
            if j % 2 == query_half:
                pieces.append(q[rs] * jnp.exp2(g[rs] - g_r))
            else:
                pieces.append(k[rs] * jnp.exp2(g_r - g[rs]))
        return jnp.concatenate(pieces, axis=0).astype(BF16)

    is_query = ((row // m) % 2) == query_half
    if m == 1:
        return jnp.where(is_query, q * jnp.exp2(lf), k).astype(BF16)
    per_tile = SUBLANES // (2 * m)
    tiles = []
    for v in range(BLOCK_A // SUBLANES):
        g_r = boundary(v * SUBLANES, SUBLANES)
        for b in range(1, per_tile):
            in_b = ((row[0:SUBLANES] % SUBLANES) // (2 * m)) == b
            g_r = jnp.where(in_b, boundary(v * SUBLANES + b * 2 * m, SUBLANES), g_r)
        tiles.append(g_r)
    g_r = jnp.concatenate(tiles, axis=0)
    return (jnp.where(is_query, q, k) * jnp.exp2(-jnp.abs(g - g_r))).astype(BF16)


def _hgrn2_scan_kernel(zq_ref, zv_ref, zf_ref, lb_ref, s0_ref, o_ref, sfin_ref,
                       st_ref, q_s, k_s, lf_s, g_s, *, reverse, tb, hpb):
    t = pl.program_id(2)
    gpad = BLOCK_A // 2
    n_blocks = tb // BLOCK_A

    @pl.when(t == 0)
    def _():
        st_ref[...] = s0_ref[...]

    zero_gpad = jnp.zeros((gpad, DK_A), F32)
    for i in range(n_blocks * hpb):
        g_s[i, 0:gpad, :] = zero_gpad
        g_s[i, gpad + BLOCK_A:gpad + BLOCK_A + gpad, :] = zero_gpad

    lb = lb_ref[...]
    zf = zf_ref[...]
    q_s[...] = _silu(zq_ref[...])
    k_s[...] = (1.0 - lb) * jax.nn.sigmoid(-zf)
    lf_s[...] = LOG2E * jnp.logaddexp(jnp.log(lb), jnp.log1p(-lb) + jax.nn.log_sigmoid(zf))

    ti = lax.broadcasted_iota(jnp.int32, (BLOCK_A, BLOCK_A), 0)
    si = lax.broadcasted_iota(jnp.int32, (BLOCK_A, BLOCK_A), 1)
    levels = [BLOCK_A >> i for i in range(1, BLOCK_A.bit_length())]
    same = {m: (ti // m) == (si // m) for m in levels}
    kill = (si <= ti) if reverse else (si >= ti)
    sgn = 1 if reverse else -1
    last = 0 if reverse else BLOCK_A - 1

    block_order = range(n_blocks - 1, -1, -1) if reverse else range(n_blocks)
    units = [(blk, h) for blk in block_order for h in range(hpb)]
    view = {u: (slice(u[0] * BLOCK_A, (u[0] + 1) * BLOCK_A), slice(u[1] * DK_A, (u[1] + 1) * DK_A))
            for u in units}

    g = {u: lf_s[view[u]] for u in units}
    shift = 1
    while shift < BLOCK_A:
        for u in units:
            g_ref = g_s.at[u[0] * hpb + u[1]]
            g_ref[gpad:gpad + BLOCK_A, :] = g[u]
            off = gpad + sgn * shift
            g[u] = g[u] + g_ref[off:off + BLOCK_A, :]
        shift *= 2
    for u in units:
        g_s[u[0] * hpb + u[1], gpad:gpad + BLOCK_A, :] = g[u]

    attn = {}
    for m in levels:
        for u in units:
            g_ref = g_s.at[u[0] * hpb + u[1]]
            x = _hgrn2_level_operand(m, q_s[view[u]], k_s[view[u]], g[u], lf_s[view[u]], g_ref, gpad,
                                     ti, reverse)
            a = _dot_nt(x, x)
            attn[u] = jnp.where(same[2 * m], a, attn[u]) if u in attn else a

    for u in units:
        h = u[1]
        q = q_s[view[u]]
        k = k_s[view[u]]
        v = zv_ref[view[u]]
        v_bf = v.astype(BF16)
        glast = g[u][last:last + 1, :]
        qhat = (q * jnp.exp2(g[u])).astype(BF16)
        khat = (k * jnp.exp2(glast - g[u])).astype(BF16)
        st = st_ref[h]
        diag = jnp.sum(q * k, axis=-1, keepdims=True)
        o_ref[view[u]] = (diag * v + _dot(jnp.where(kill, 0.0, attn[u]).astype(BF16), v_bf)
                          + _dot_nt(qhat, st.astype(BF16)))
        st_ref[h] = st * jnp.exp2(glast) + _dot_tn(v_bf, khat)

    @pl.when(t == pl.num_programs(2) - 1)
    def _():
        sfin_ref[...] = st_ref[...]


def _hgrn2_scan(z, lb, s0, *, reverse, tb=256):
    bsz, t_len, _ = z.shape
    heads = s0.shape[1]
    hpb = HEADS_PER_BLOCK_A
    width = hpb * DK_A
    n_hb = heads // hpb
    tb = min(tb, t_len)
    nt = t_len // tb
    tt = (lambda t: nt - 1 - t) if reverse else (lambda t: t)
    f_base = (4 if reverse else 3) * n_hb
    kern = functools.partial(_hgrn2_scan_kernel, reverse=reverse, tb=tb, hpb=hpb)
    state_spec = pl.BlockSpec((None, hpb, DK_A, DK_A), lambda b, hb, t: (b, hb, 0, 0))
    return pl.pallas_call(
        kern,
        grid=(bsz, n_hb, nt),
        in_specs=[pl.BlockSpec((None, tb, width), lambda b, hb, t: (b, tt(t), hb)),
                  pl.BlockSpec((None, tb, width), lambda b, hb, t: (b, tt(t), 2 * n_hb + hb)),
                  pl.BlockSpec((None, tb, width), lambda b, hb, t: (b, tt(t), f_base + hb)),
                  pl.BlockSpec((1, width), lambda b, hb, t: (0, hb)),
                  state_spec],
        out_specs=[pl.BlockSpec((None, tb, width), lambda b, hb, t: (b, tt(t), hb)),
                   state_spec],
        out_shape=[jax.ShapeDtypeStruct((bsz, t_len, heads * DK_A), F32),
                   jax.ShapeDtypeStruct(s0.shape, F32)],
        scratch_shapes=[pltpu.VMEM((hpb, DK_A, DK_A), F32),
                        pltpu.VMEM((tb, width), F32),
                        pltpu.VMEM((tb, width), F32),
                        pltpu.VMEM((tb, width), F32),
                        pltpu.VMEM((tb // BLOCK_A * hpb, 2 * BLOCK_A, DK_A), F32)],
        compiler_params=_params("parallel", "parallel", "arbitrary"),
        name="hgrn2_scan_bwd" if reverse else "hgrn2_scan_fwd",
    )(z, z, z, lb, s0)


def _hgrn2_out_proj_kernel(of_ref, ob_ref, zg_ref, on_ref, w_ref, h_ref, mod_ref, g_ref, o_ref,
                           *, heads):
    pieces = []
    for hd in range(heads):
        hs = slice(hd * DK_A, (hd + 1) * DK_A)
        o = of_ref[:, hs] + ob_ref[:, hs]
        y = _rms(o) * on_ref[:, hs]
        pieces.append((y * _silu(zg_ref[:, hs])).astype(BF16))
    a = jnp.concatenate(pieces, axis=1)
    y = _rms(_dot(a, w_ref[...])) * g_ref[1:2, :]
    o_ref[...] = h_ref[...] + mod_ref[2:3, :] * y


def _hgrn2_out_proj(o_f, o_b, z2d, onorm, w, h, mod, g4, *, rows_per_mod, tm=256):
    m, d_a = o_f.shape
    d = w.shape[1]
    tm = min(tm, m, rows_per_mod)
    tiles_per_mod = rows_per_mod // tm
    kern = functools.partial(_hgrn2_out_proj_kernel, heads=d_a // DK_A)
    return pl.pallas_call(
        kern,
        grid=(m // tm,),
        in_specs=[pl.BlockSpec((tm, d_a), lambda i: (i, 0)),
                  pl.BlockSpec((tm, d_a), lambda i: (i, 0)),
                  pl.BlockSpec((tm, d_a), lambda i: (i, 1)),
                  pl.BlockSpec((1, d_a), lambda i: (0, 0)),
                  pl.BlockSpec((d_a, d), lambda i: (0, 0)),
                  pl.BlockSpec((tm, d), lambda i: (i, 0)),
                  pl.BlockSpec((None, 6, d), lambda i: (i // tiles_per_mod, 0, 0)),
                  pl.BlockSpec((4, d), lambda i: (0, 0))],
        out_specs=pl.BlockSpec((tm, d), lambda i: (i, 0)),
        out_shape=jax.ShapeDtypeStruct((m, d), F32),
        compiler_params=_params("parallel"),
        name="hgrn2_out_proj_residual",
    )(o_f, o_b, z2d, onorm, w, h, mod, g4)


def _rope_kernel(x_ref, cos_ref, sin_ref, o_ref, *, n_q, n_rot):
    cos = cos_ref[...]
    sin = sin_ref[...]
    lane = lax.broadcasted_iota(jnp.int32, cos.shape, 1)
    first_half = (lane % (DH_B // 2)) < ROPE_PAIRS
    for h in range(x_ref.shape[1] // DH_B):
        hs = slice(h * DH_B, (h + 1) * DH_B)
        x = x_ref[:, hs]
        if h < n_rot:
            partner = jnp.where(first_half, pltpu.roll(x, DH_B - ROPE_PAIRS, 1),
                                pltpu.roll(x, ROPE_PAIRS, 1))
            x = x * cos + partner * sin
        if h < n_q:
            x = x * SCALE_B
        o_ref[:, hs] = x.astype(BF16)


def _rope_cast(qkv, cos, sin, *, n_q, n_rot, tm=512):
    bsz, t_len, w = qkv.shape
    tm = min(tm, t_len)
    kern = functools.partial(_rope_kernel, n_q=n_q, n_rot=n_rot)
    return pl.pallas_call(
        kern,
        grid=(bsz, t_len // tm),
        in_specs=[pl.BlockSpec((None, tm, w), lambda b, i: (b, i, 0)),
                  pl.BlockSpec((tm, DH_B), lambda b, i: (i, 0)),
                  pl.BlockSpec((tm, DH_B), lambda b, i: (i, 0))],
        out_specs=pl.BlockSpec((None, tm, w), lambda b, i: (b, i, 0)),
        out_shape=jax.ShapeDtypeStruct((bsz, t_len, w), BF16),
        compiler_params=_params("parallel", "parallel"),
        name="rope_cast",
    )(qkv, cos, sin)


def _sink_column(sink_ref, kv, rows_per_head):
    return jnp.concatenate(
        [jnp.full((rows_per_head, 1), sink_ref[kv, g], F32) for g in range(GRP_B)], axis=0)


def _stack_heads(q):
    return jnp.concatenate([q[:, g * DH_B:(g + 1) * DH_B] for g in range(GRP_B)], axis=0)


def _with_ones(v):
    return jnp.concatenate([v, jnp.ones_like(v)], axis=1)


def _row_max(*scores):
    tiles = [s[:, t:t + DH_B] for s in scores for t in range(0, s.shape[1], DH_B)]
    return jnp.max(functools.reduce(jnp.maximum, tiles), axis=-1, keepdims=True)


def _swa_kernel(sink_ref, q_ref, kp_ref, kc_ref, kn_ref, vp_ref, vc_ref, vn_ref, kx_ref, vx_ref,
                o_ref, *, nq):
    kv = pl.program_id(1)
    j = pl.program_id(2)
    nj = pl.num_programs(2)
    kb = jnp.concatenate([kp_ref[...], kc_ref[...], kn_ref[...]], axis=0)
    vb = _with_ones(jnp.concatenate([vp_ref[...], vc_ref[...], vn_ref[...]], axis=0))
    kx = kx_ref[...]
    vx = _with_ones(vx_ref[...])
    sink = _sink_column(sink_ref, kv, BLK_B)
    shape = (GRP_B * BLK_B, 3 * BLK_B)
    r = lax.broadcasted_iota(jnp.int32, shape, 0) % BLK_B
    c = lax.broadcasted_iota(jnp.int32, shape, 1)
    in_window = jnp.abs(c - BLK_B - r) <= WINDOW_B
    for i in range(nq):
        rows = slice(i * BLK_B, (i + 1) * BLK_B)
        qs = _stack_heads(q_ref[rows, :])
        s_loc = _dot_nt(qs, kb[i * BLK_B:(i + 3) * BLK_B])
        valid = in_window
        if i == 0:
            valid = valid & ((c >= BLK_B) | (j > 0))
        if i == nq - 1:
            valid = valid & ((c < 2 * BLK_B) | (j < nj - 1))
        s_loc = jnp.where(valid, s_loc, -jnp.inf)
        s_ctx = _dot_nt(qs, kx)
        m = jnp.maximum(_row_max(s_loc, s_ctx), sink)
        p_loc = jnp.exp(s_loc - m)
        p_ctx = jnp.exp(s_ctx - m)
        acc = (_dot(p_loc.astype(BF16), vb[i * BLK_B:(i + 3) * BLK_B])
               + _dot(p_ctx.astype(BF16), vx))
        o = acc[:, :DH_B] / (acc[:, DH_B:DH_B + 1] + jnp.exp(sink - m))
        for g in range(GRP_B):
            o_ref[rows, g * DH_B:(g + 1) * DH_B] = o[g * BLK_B:(g + 1) * BLK_B].astype(BF16)


def _swa_attention(sink, qkv, qkv_ctx, *, n_q, n_kv):
    bsz, t_len, _ = qkv.shape
    l_len = qkv_ctx.shape[1]
    nb = t_len // BLK_B
    nq = math.gcd(nb, 4)
    gw = GRP_B * DH_B
    k0, v0 = n_q, n_q + n_kv

    def halo(col0, edge):
        return pl.BlockSpec(
            (None, BLK_B, DH_B),
            lambda b, kv, j: (b, jnp.clip(j * nq + edge, 0, nb - 1), col0 + kv))

    def cur(col0):
        return pl.BlockSpec((None, nq * BLK_B, DH_B), lambda b, kv, j: (b, j, col0 + kv))

    return pl.pallas_call(
        functools.partial(_swa_kernel, nq=nq),
        grid=(bsz, n_kv, nb // nq),
        in_specs=[pl.BlockSpec(memory_space=pltpu.SMEM),
                  pl.BlockSpec((None, nq * BLK_B, gw), lambda b, kv, j: (b, j, kv)),
                  halo(k0, -1), cur(k0), halo(k0, nq),
                  halo(v0, -1), cur(v0), halo(v0, nq),
                  pl.BlockSpec((None, l_len, DH_B), lambda b, kv, j: (b, 0, k0 + kv)),
                  pl.BlockSpec((None, l_len, DH_B), lambda b, kv, j: (b, 0, v0 + kv))],
        out_specs=pl.BlockSpec((None, nq * BLK_B, gw), lambda b, kv, j: (b, j, kv)),
        out_shape=jax.ShapeDtypeStruct((bsz, t_len, n_q * DH_B), BF16),
        compiler_params=_params("parallel", "parallel", "parallel"),
        name="swa_attention",
    )(sink, qkv, qkv, qkv, qkv, qkv, qkv, qkv, qkv_ctx, qkv_ctx)


def _ctx_attention_kernel(sink_ref, q_ref, kx_ref, vx_ref, o_ref):
    kv = pl.program_id(1)
    l_len = q_ref.shape[0]
    qs = _stack_heads(q_ref[...])
    s = _dot_nt(qs, kx_ref[...])
    sink = _sink_column(sink_ref, kv, l_len)
    m = jnp.maximum(jnp.max(s, axis=-1, keepdims=True), sink)
    p = jnp.exp(s - m)
    den = jnp.sum(p, axis=-1, keepdims=True) + jnp.exp(sink - m)
    o = _dot(p.astype(BF16), vx_ref[...]) / den
    for g in range(GRP_B):
        o_ref[:, g * DH_B:(g + 1) * DH_B] = o[g * l_len:(g + 1) * l_len].astype(BF16)


def _ctx_attention(sink, qkv_ctx, *, n_q, n_kv):
    bsz, l_len, _ = qkv_ctx.shape
    gw = GRP_B * DH_B
    return pl.pallas_call(
        _ctx_attention_kernel,
        grid=(bsz, n_kv),
        in_specs=[pl.BlockSpec(memory_space=pltpu.SMEM),
                  pl.BlockSpec((None, l_len, gw), lambda b, kv: (b, 0, kv)),
                  pl.BlockSpec((None, l_len, DH_B), lambda b, kv: (b, 0, n_q + kv)),
                  pl.BlockSpec((None, l_len, DH_B), lambda b, kv: (b, 0, n_q + n_kv + kv))],
        out_specs=pl.BlockSpec((None, l_len, gw), lambda b, kv: (b, 0, kv)),
        out_shape=jax.ShapeDtypeStruct((bsz, l_len, n_q * DH_B), BF16),
        compiler_params=_params("parallel", "parallel"),
        name="ctx_attention",
    )(sink, qkv_ctx, qkv_ctx, qkv_ctx)


def _conv_silu_kernel(prev_ref, cur_ref, next_ref, w_ref, b_ref, o_ref, ext_ref, *, tb):
    i = pl.program_id(1)
    halo = 8
    has_prev = (i > 0).astype(F32)
    has_next = (i < pl.num_programs(1) - 1).astype(F32)
    ext_ref[0:halo, :] = prev_ref[...] * has_prev
    ext_ref[halo:halo + tb, :] = cur_ref[...]
    ext_ref[halo + tb:halo + tb + halo, :] = next_ref[...] * has_next
    acc = jnp.zeros(o_ref.shape, F32) + b_ref[...]
    for tap in range(CONV_C):
        off = halo + tap - CONV_C // 2
        acc = acc + ext_ref[off:off + tb, :] * w_ref[tap:tap + 1, :]
    o_ref[...] = _silu(acc)


def _conv_silu(z, conv_w, conv_b, *, col0, tb=512, tc=1024):
    bsz, t_len, _ = z.shape
    ch = conv_w.shape[1]
    tb = min(tb, t_len)
    halo = 8
    hb = tb // halo
    n_h = t_len // halo
    c0 = col0 // tc
    kern = functools.partial(_conv_silu_kernel, tb=tb)
    return pl.pallas_call(
        kern,
        grid=(bsz, t_len // tb, ch // tc),
        in_specs=[pl.BlockSpec((None, halo, tc), lambda b, i, c: (b, jnp.maximum(i * hb - 1, 0), c0 + c)),
                  pl.BlockSpec((None, tb, tc), lambda b, i, c: (b, i, c0 + c)),
                  pl.BlockSpec((None, halo, tc), lambda b, i, c: (b, jnp.minimum((i + 1) * hb, n_h - 1), c0 + c)),
                  pl.BlockSpec((CONV_C, tc), lambda b, i, c: (0, c)),
                  pl.BlockSpec((1, tc), lambda b, i, c: (0, c))],
        out_specs=pl.BlockSpec((None, tb, tc), lambda b, i, c: (b, i, c)),
        out_shape=jax.ShapeDtypeStruct((bsz, t_len, ch), F32),
        scratch_shapes=[pltpu.VMEM((tb + 2 * halo, tc), F32)],
        compiler_params=_params("parallel", "parallel", "parallel"),
        name="conv_silu",
    )(z, z, z, conv_w, conv_b)


def _split_bf16(x, parts):
    pieces = []
    for _ in range(parts):
        p = x.astype(BF16)
        pieces.append(p)
        x = x - p.astype(F32)
    return jnp.concatenate(pieces, axis=1)


def _rep_matrix(heads, width, parts):
    r = jnp.repeat(jnp.eye(heads, dtype=BF16), width, axis=1)
    return jnp.concatenate([r] * parts, axis=0)


def _ssd_scan_kernel(x_ref, b_ref, c_ref, dtr_ref, dtb_ref, an_ref, rep_n_ref, rep_p_ref, h0_ref,
                     y_ref, hfin_ref, st_ref, *, reverse, groups):
    t = pl.program_id(1)
    cl = CHUNK_C
    heads = x_ref.shape[1] // P_C
    hpg = heads // groups
    lane0 = heads if reverse else 0

    @pl.when(t == 0)
    def _():
        st_ref[...] = h0_ref[...]

    dt2 = jnp.logaddexp(dtr_ref[...] + dtb_ref[...], 0.0)
    la2 = dt2 * (an_ref[...] * LOG2E)
    ri = lax.broadcasted_iota(jnp.int32, (cl, cl), 0)
    ci = lax.broadcasted_iota(jnp.int32, (cl, cl), 1)
    causal = (ci >= ri) if reverse else (ci <= ri)
    cum2 = jnp.dot(causal.astype(F32), la2, precision=lax.Precision.HIGHEST,
                   preferred_element_type=F32)
    cum_t = cum2.T[lane0:lane0 + heads, :]
    cum = cum2[:, lane0:lane0 + heads]
    dt = dt2[:, lane0:lane0 + heads]
    last = 0 if reverse else cl - 1
    cum_pieces = _split_bf16(cum, CUM_PARTS_C)
    dt_pieces = _split_bf16(dt, DT_PARTS_C)
    lane = lax.broadcasted_iota(jnp.int32, (cl, 2 * P_C), 1)
    lo = lane < P_C

    for g0 in range(0, groups, GROUPS_PER_STAGE_C):
        stage_groups = range(g0, g0 + GROUPS_PER_STAGE_C)
        bm = {g: b_ref[:, g * N_STATE_C:(g + 1) * N_STATE_C].astype(BF16) for g in stage_groups}
        cm = {g: c_ref[:, g * N_STATE_C:(g + 1) * N_STATE_C].astype(BF16) for g in stage_groups}
        cb = {g: _dot_nt(cm[g], bm[g]) for g in stage_groups}
        pairs = [(g, g * hpg + 2 * p) for g in stage_groups for p in range(hpg // 2)]
        xds, cums, outs = [], [], []
        for _, h_a in pairs:
            xs = slice(h_a * P_C, (h_a + 2) * P_C)
            ns = slice(h_a * N_STATE_C, (h_a + 2) * N_STATE_C)
            xds.append(x_ref[:, xs] * _dot(dt_pieces, rep_p_ref[:, xs]))
            cums.append(_dot(cum_pieces, rep_n_ref[:, ns]))
        for (g, h_a), xd, cum_n in zip(pairs, xds, cums):
            xd_bf = xd.astype(BF16)
            ys = []
            for i in range(2):
                diff = cum_n[:, i * N_STATE_C:(i + 1) * N_STATE_C] - cum_t[h_a + i:h_a + i + 1, :]
                seg = jnp.exp2(jnp.where(causal, diff, -jnp.inf))
                ys.append(_dot((cb[g] * seg).astype(BF16), xd_bf))
            outs.append(jnp.where(lo, ys[0], ys[1]))
        for (g, h_a), xd, cum_n, y_intra in zip(pairs, xds, cums, outs):
            xs = slice(h_a * P_C, (h_a + 2) * P_C)
            cum_p = jnp.where(lo, cum_n[:, :N_STATE_C], cum_n[:, N_STATE_C:])
            tot_p = cum_p[last:last + 1, :]
            st = st_ref[h_a // 2]
            y_ref[:, xs] = y_intra + jnp.exp2(cum_p) * _dot(cm[g], st.astype(BF16))
            st_ref[h_a // 2] = (st * jnp.exp2(tot_p)
                                + _dot_tn(bm[g], (xd * jnp.exp2(tot_p - cum_p)).astype(BF16)))

    @pl.when(t == pl.num_programs(1) - 1)
    def _():
        hfin_ref[...] = st_ref[...]


def _ssd_scan(xbc, z, dt_bias, a_neg, h0, *, reverse, d_inner, groups, dt_col0):
    bsz, t_len, _ = xbc.shape
    n_pairs = h0.shape[1]
    heads = 2 * n_pairs
    nc = t_len // CHUNK_C
    gn = groups * N_STATE_C
    two_h = dt_bias.shape[1]
    tt = (lambda t: nc - 1 - t) if reverse else (lambda t: t)
    kern = functools.partial(_ssd_scan_kernel, reverse=reverse, groups=groups)
    state_spec = pl.BlockSpec((None, n_pairs, N_STATE_C, 2 * P_C), lambda b, t: (b, 0, 0, 0))
    rep_n = _rep_matrix(heads, N_STATE_C, CUM_PARTS_C)
    rep_p = _rep_matrix(heads, P_C, DT_PARTS_C)
    return pl.pallas_call(
        kern,
        grid=(bsz, nc),
        in_specs=[pl.BlockSpec((None, CHUNK_C, d_inner), lambda b, t: (b, tt(t), 0)),
                  pl.BlockSpec((None, CHUNK_C, gn), lambda b, t: (b, tt(t), d_inner // gn)),
                  pl.BlockSpec((None, CHUNK_C, gn), lambda b, t: (b, tt(t), d_inner // gn + 1)),
                  pl.BlockSpec((None, CHUNK_C, two_h), lambda b, t: (b, tt(t), dt_col0 // two_h)),
                  pl.BlockSpec((1, two_h), lambda b, t: (0, 0)),
                  pl.BlockSpec((1, two_h), lambda b, t: (0, 0)),
                  pl.BlockSpec(rep_n.shape, lambda b, t: (0, 0)),
                  pl.BlockSpec(rep_p.shape, lambda b, t: (0, 0)),
                  state_spec],
        out_specs=[pl.BlockSpec((None, CHUNK_C, d_inner), lambda b, t: (b, tt(t), 0)),
                   state_spec],
        out_shape=[jax.ShapeDtypeStruct((bsz, t_len, d_inner), F32),
                   jax.ShapeDtypeStruct(h0.shape, F32)],
        scratch_shapes=[pltpu.VMEM(h0.shape[1:], F32)],
        compiler_params=_params("parallel", "arbitrary"),
        name="ssd_scan_bwd" if reverse else "ssd_scan_fwd",
    )(xbc, xbc, xbc, z, dt_bias, a_neg, rep_n, rep_p, h0)


def _ssd_readout_kernel(yf_ref, yb_ref, x_ref, z_ref, d_ref, on_ref, a_ref, *, groups):
    gw = a_ref.shape[1] // groups
    for g in range(groups):
        gs = slice(g * gw, (g + 1) * gw)
        y = (yf_ref[:, gs] + yb_ref[:, gs] + d_ref[:, gs] * x_ref[:, gs]) * _silu(z_ref[:, gs])
        a_ref[:, gs] = (_rms(y) * on_ref[:, gs]).astype(BF16)


def _ssd_readout(y_f, y_b, xbc2d, z2d, d_exp, onorm, *, groups, tm=256):
    m, d_inner = y_f.shape
    tm = min(tm, m)
    kern = functools.partial(_ssd_readout_kernel, groups=groups)
    row = lambda i: (i, 0)
    return pl.pallas_call(
        kern,
        grid=(m // tm,),
        in_specs=[pl.BlockSpec((tm, d_inner), row),
                  pl.BlockSpec((tm, d_inner), row),
                  pl.BlockSpec((tm, d_inner), row),
                  pl.BlockSpec((tm, d_inner), row),
                  pl.BlockSpec((1, d_inner), lambda i: (0, 0)),
                  pl.BlockSpec((1, d_inner), lambda i: (0, 0))],
        out_specs=pl.BlockSpec((tm, d_inner), row),
        out_shape=jax.ShapeDtypeStruct((m, d_inner), BF16),
        compiler_params=_params("parallel"),
        name="ssd_readout",
    )(y_f, y_b, xbc2d, z2d, d_exp, onorm)


def _in_proj(h, mod, g4, w, *, rows_per_mod, tn):
    return _modlinear(h, mod, g4, w, g_row=0, shift_row=0, scale_row=1,
                      rows_per_mod=rows_per_mod, tn=tn)


def _hgrn2_mixer(zc, zl, lb, onorm, emit_ctx):
    bsz, t_len, five_d = zl.shape
    d_a = five_d // 5
    heads = d_a // DK_A
    l_len = zc.shape[1]
    s0 = jnp.zeros((bsz, heads, DK_A, DK_A), F32)
    o_lat, o_ctx = [], []
    for d in range(2):
        lb_d = lb[d].reshape(1, d_a)
        oc, s_ctx = _hgrn2_scan(zc, lb_d, s0, reverse=bool(d))
        ol, _ = _hgrn2_scan(zl, lb_d, s_ctx, reverse=bool(d))
        o_ctx.append(oc)
        o_lat.append(ol)
    on = onorm.reshape(1, d_a)
    al = (o_lat[0].reshape(bsz * t_len, d_a), o_lat[1].reshape(bsz * t_len, d_a),
          zl.reshape(bsz * t_len, five_d), on)
    ac = None
    if emit_ctx:
        ac = (o_ctx[0].reshape(bsz * l_len, d_a), o_ctx[1].reshape(bsz * l_len, d_a),
              zc.reshape(bsz * l_len, five_d), on)
    return ac, al


def _rope_tables(t_len):
    rows = t_len // GRID_W
    inv_freq = ROPE_BASE ** (-jnp.arange(ROPE_PAIRS, dtype=F32) / ROPE_PAIRS)
    row = jnp.repeat(jnp.arange(rows, dtype=F32), GRID_W)
    col = jnp.tile(jnp.arange(GRID_W, dtype=F32), rows)
    ang_row = row[:, None] * inv_freq
    ang_col = col[:, None] * inv_freq
    cos = jnp.concatenate([jnp.cos(ang_row)] * 2 + [jnp.cos(ang_col)] * 2, axis=-1)
    sin = jnp.concatenate([-jnp.sin(ang_row), jnp.sin(ang_row),
                           -jnp.sin(ang_col), jnp.sin(ang_col)], axis=-1)
    return cos, sin


def _swa_mixer(zc, zl, sink, emit_ctx):
    bsz, t_len, w = zl.shape
    l_len = zc.shape[1]
    n_kv = sink.shape[0] // GRP_B
    n_q = sink.shape[0]
    cos, sin = _rope_tables(t_len)
    ql = _rope_cast(zl, cos, sin, n_q=n_q, n_rot=n_q + n_kv)
    ones = jnp.ones((l_len, DH_B), F32)
    qc = _rope_cast(zc, ones, ones, n_q=n_q, n_rot=0)
    sink2 = sink.astype(F32).reshape(n_kv, GRP_B)
    al = _swa_attention(sink2, ql, qc, n_q=n_q, n_kv=n_kv).reshape(bsz * t_len, n_q * DH_B)
    ac = None
    if emit_ctx:
        ac = _ctx_attention(sink2, qc, n_q=n_q, n_kv=n_kv).reshape(bsz * l_len, n_q * DH_B)
    return ac, al


def _ssd_mixer(zc, zl, conv_w, conv_b, dt_bias, a_log, d_skip, onorm, emit_ctx):
    bsz = zl.shape[0]
    d_inner = onorm.shape[0]
    heads = d_skip.shape[0]
    groups = N_GROUPS_C
    gn = groups * N_STATE_C
    a_neg = (-jnp.exp(a_log.astype(F32))).reshape(1, 2 * heads)
    dtb = dt_bias.astype(F32).reshape(1, 2 * heads)
    d_exp = jnp.repeat(d_skip.astype(F32), P_C).reshape(1, d_inner)
    on = onorm.reshape(1, d_inner)
    cb = conv_b.reshape(1, -1)
    xbc_c = _conv_silu(zc, conv_w, cb, col0=d_inner)
    xbc_l = _conv_silu(zl, conv_w, cb, col0=d_inner)
    h0 = jnp.zeros((bsz, heads // 2, N_STATE_C, 2 * P_C), F32)
    kw = dict(d_inner=d_inner, groups=groups, dt_col0=2 * d_inner + 2 * gn)
    y_lat, y_ctx = [], []
    for d in range(2):
        yc, h_ctx = _ssd_scan(xbc_c, zc, dtb, a_neg, h0, reverse=bool(d), **kw)
        yl, _ = _ssd_scan(xbc_l, zl, dtb, a_neg, h_ctx, reverse=bool(d), **kw)
        y_ctx.append(yc)
        y_lat.append(yl)

    def readout(ys, xbc, z):
        m = z.shape[0] * z.shape[1]
        return _ssd_readout(ys[0].reshape(m, d_inner), ys[1].reshape(m, d_inner),
                            xbc.reshape(m, -1), z.reshape(m, -1), d_exp, on, groups=groups)

    ac = readout(y_ctx, xbc_c, zc) if emit_ctx else None
    return ac, readout(y_lat, xbc_l, zl)


def kernel(x, c, ctx, c_ctx, ada_w, ada_b, norm_g, mlp_w1, mlp_w2, a_w_in, a_lb_logits, a_onorm,
           a_w_out, b_w_qkv, b_sink, b_w_out, c_w_in, c_conv_w, c_conv_b, c_dt_bias, c_a_log, c_d,
           c_onorm, c_w_out):
    bsz, t_len, d = x.shape
    l_len = ctx.shape[1]
    depth = ada_w.shape[0]

    p_lb = jax.nn.softmax(a_lb_logits.astype(F32), axis=0)
    lower_bounds = jnp.cumsum(p_lb, axis=0) - p_lb

    c_rows = jnp.concatenate([c, c_ctx[None, :], jnp.zeros((8 - bsz - 1, d), F32)], axis=0)
    mods = _ada(c_rows, ada_w, ada_b).reshape(depth, 8, 6, d)

    hl = x.reshape(bsz * t_len, d)
    hc = ctx.reshape(bsz * l_len, d)
    rows_l, rows_c = t_len, bsz * l_len
    for i in range(depth):
        emit_ctx = i < depth - 1
        m_l = mods[i, :bsz]
        m_c = mods[i, bsz:bsz + 1]
        g4 = norm_g[i]
        kind, j = i % N_MIXERS, i // N_MIXERS
        if kind == 0:
            w_in, w_out, tn = a_w_in[j].astype(BF16), a_w_out[j].astype(BF16), 1024
        elif kind == 1:
            w_in, w_out, tn = b_w_qkv[j].astype(BF16), b_w_out[j].astype(BF16), 1024
        else:
            w_in, w_out, tn = c_w_in[j].astype(BF16), c_w_out[j].astype(BF16), 1152
        zl = _in_proj(hl, m_l, g4, w_in, rows_per_mod=rows_l, tn=tn).reshape(bsz, t_len, -1)
        zc = _in_proj(hc, m_c, g4, w_in, rows_per_mod=rows_c, tn=tn).reshape(bsz, l_len, -1)
        if kind == 0:
            ac, al = _hgrn2_mixer(zc, zl, lower_bounds[i], a_onorm[j], emit_ctx)
        elif kind == 1:
            ac, al = _swa_mixer(zc, zl, b_sink[j], emit_ctx)
        else:
            ac, al = _ssd_mixer(zc, zl, c_conv_w[j], c_conv_b[j], c_dt_bias[j], c_a_log[j], c_d[j],
                                c_onorm[j], emit_ctx)
        w1 = mlp_w1[i].astype(BF16)
        w2 = mlp_w2[i].astype(BF16)
        if kind == 0:
            out_proj = lambda a, h, m, rows: _hgrn2_out_proj(*a, w_out, h, m, g4, rows_per_mod=rows)
        else:
            out_proj = lambda a, h, m, rows: _linear_resnorm(a, w_out, h, m, g4, g_row=1, gate_row=2,
                                                             rows_per_mod=rows)
        hl = out_proj(al, hl, m_l, rows_l)
        hl = _mlp(hl, m_l, g4, w1, w2, rows_per_mod=rows_l)
        if emit_ctx:
            hc = out_proj(ac, hc, m_c, rows_c)
            hc = _mlp(hc, m_c, g4, w1, w2, rows_per_mod=rows_c)
    return hl.reshape(bsz, t_len, d)
```
